```python
import jax, jax.numpy as jnp
from jax import lax
import numpy as np

D_MODEL = 1024
BATCH = 16
SEQ = 2048
DEPTH = 4

N_HEADS = 8
QK_NOPE_DIM = 128
QK_ROPE_DIM = 64
QK_DIM = QK_NOPE_DIM + QK_ROPE_DIM
V_HEAD_DIM = 128
Q_LORA_RANK = 384
KV_LORA_RANK = 256
ROPE_THETA = 10000.0
Q_BLOCK = 128
POOL_WINDOWS = (2, 4, 8, 16)
N_POOL_GROUPS = 4
POOL_GROUP_DIM = 128
POOL_DIM = N_POOL_GROUPS * POOL_GROUP_DIM
N_BRANCHES = 2
IN_DIM = POOL_DIM + Q_LORA_RANK + KV_LORA_RANK + QK_ROPE_DIM + N_BRANCHES * D_MODEL
SPLIT_POINTS = (POOL_DIM,
                POOL_DIM + Q_LORA_RANK,
                POOL_DIM + Q_LORA_RANK + KV_LORA_RANK,
                POOL_DIM + Q_LORA_RANK + KV_LORA_RANK + QK_ROPE_DIM)
D_FF = 2816
NORM_EPS = 1e-6

kernel_name = "macaron_gated_pool_mla_trunk"


def rms_norm(x, g):
    xf = x.astype(jnp.float32)
    y = xf * lax.rsqrt(jnp.mean(xf * xf, axis=-1, keepdims=True) + NORM_EPS)
    return (y * g.astype(jnp.float32)).astype(x.dtype)


def swiglu_ffn(h, w_up, w_down):
    gate, up = jnp.split(h @ w_up, 2, axis=-1)
    return (jax.nn.silu(gate) * up) @ w_down


def rope_tables(positions):
    inv_freq = ROPE_THETA ** (-jnp.arange(0, QK_ROPE_DIM, 2, dtype=jnp.float32) / QK_ROPE_DIM)
    ang = positions.astype(jnp.float32)[..., None] * inv_freq
    return jnp.cos(ang), jnp.sin(ang)


def apply_rope(x, cos, sin):
    xf = x.astype(jnp.float32)
    x1, x2 = jnp.split(xf, 2, axis=-1)
    out = jnp.concatenate([x1 * cos - x2 * sin, x2 * cos + x1 * sin], axis=-1)
    return out.astype(x.dtype)


def causal_multiscale_pool(xp, pool_maps, pool_scale):
    B, S, _ = xp.shape
    xg = xp.reshape(B, S, N_POOL_GROUPS, POOL_GROUP_DIM).astype(jnp.float32)
    csum = jnp.pad(jnp.cumsum(xg, axis=1), ((0, 0), (1, 0), (0, 0), (0, 0)))
    windows = jnp.array(POOL_WINDOWS, dtype=jnp.int32)
    t = jnp.arange(S, dtype=jnp.int32)[:, None]
    start = jnp.maximum(t + 1 - windows[None, :], 0)
    csum_start = csum[:, start, jnp.arange(N_POOL_GROUPS)[None, :]]
    count = jnp.minimum(t + 1, windows[None, :]).astype(jnp.float32)
    pooled = (csum[:, 1:] - csum_start) / count[None, :, :, None] - xg
    mixed = jnp.einsum('bsgc,gcd->bsgd', pooled.astype(xp.dtype), pool_maps)
    return mixed.reshape(B, S, POOL_DIM) * pool_scale


def mla_attention(q_lat, kv_lat, k_rope, cos, sin, q_norm, w_uq, kv_norm, w_ukv):
    B, S, _ = q_lat.shape
    q = (rms_norm(q_lat, q_norm) @ w_uq).reshape(B, S, N_HEADS, QK_DIM)
    q_nope, q_rope = q[..., :QK_NOPE_DIM], q[..., QK_NOPE_DIM:]
    kv = (rms_norm(kv_lat, kv_norm) @ w_ukv).reshape(B, S, N_HEADS, QK_NOPE_DIM + V_HEAD_DIM)
    k_nope, v = kv[..., :QK_NOPE_DIM], kv[..., QK_NOPE_DIM:]
    q_rope = apply_rope(q_rope, cos[:, :, None, :], sin[:, :, None, :])
    k_rope = apply_rope(k_rope, cos, sin)
    scale = QK_DIM ** -0.5
    outs = []
    for blk in range(S // Q_BLOCK):
        q0 = blk * Q_BLOCK
        k_end = q0 + Q_BLOCK
        s = (jnp.einsum('bqhd,bkhd->bhqk', q_nope[:, q0:k_end], k_nope[:, :k_end])
             + jnp.einsum('bqhr,bkr->bhqk', q_rope[:, q0:k_end], k_rope[:, :k_end]))
        s = s.astype(jnp.float32) * scale
        mask = (q0 + jnp.arange(Q_BLOCK))[:, None] >= jnp.arange(k_end)[None, :]
        s = jnp.where(mask[None, None], s, jnp.finfo(jnp.float32).min)
        p = jax.nn.softmax(s, axis=-1).astype(v.dtype)
        outs.append(jnp.einsum('bhqk,bkhd->bqhd', p, v[:, :k_end]))
    o = jnp.concatenate(outs, axis=1)
    return o.reshape(B, S, N_HEADS * V_HEAD_DIM)


def setup_inputs(seed: int = 0) -> dict:
    key = jax.random.key(seed)
    ks = jax.random.split(key, 24)
    f32 = jnp.float32

    def w(k, shape, fan_in):
        return jax.random.normal(k, shape, f32) * (fan_in ** -0.5)

    def gain(k, shape):
        return 1.0 + 0.02 * jax.random.normal(k, shape, f32)

    L, D = DEPTH, D_MODEL
    offsets = jax.random.randint(ks[1], (BATCH, 1), 0, 4096, dtype=jnp.int32)
    positions = offsets + jnp.arange(SEQ, dtype=jnp.int32)[None, :]
    return {
        "x": jax.random.normal(ks[0], (BATCH, SEQ, D), f32),
        "positions": positions,
        "norm_ffn1": gain(ks[2], (L, D)),
        "ffn1_up": w(ks[3], (L, D, 2 * D_FF), D),
        "ffn1_down": w(ks[4], (L, D_FF, D), D_FF),
        "norm_mix": gain(ks[5], (L, D)),
        "w_in": w(ks[6], (L, D, IN_DIM), D),
        "b_gate": 0.02 * jax.random.normal(ks[7], (L, N_BRANCHES * D), f32),
        "pool_maps": w(ks[8], (L, N_POOL_GROUPS, POOL_GROUP_DIM, POOL_GROUP_DIM), POOL_GROUP_DIM),
        "pool_scale": 1.0 + 0.1 * jax.random.normal(ks[9], (L, POOL_DIM), f32),
        "w_pool_proj": w(ks[10], (L, POOL_DIM, D), POOL_DIM),
        "q_latent_norm": gain(ks[11], (L, Q_LORA_RANK)),
        "w_uq": w(ks[12], (L, Q_LORA_RANK, N_HEADS * QK_DIM), Q_LORA_RANK),
        "kv_latent_norm": gain(ks[13], (L, KV_LORA_RANK)),
        "w_ukv": w(ks[14], (L, KV_LORA_RANK, N_HEADS * (QK_NOPE_DIM + V_HEAD_DIM)), KV_LORA_RANK),
        "w_attn_proj": w(ks[15], (L, N_HEADS * V_HEAD_DIM, D), N_HEADS * V_HEAD_DIM),
        "w_out": w(ks[16], (L, D, D), D),
        "norm_ffn2": gain(ks[17], (L, D)),
        "ffn2_up": w(ks[18], (L, D, 2 * D_FF), D),
        "ffn2_down": w(ks[19], (L, D_FF, D), D_FF),
        "final_norm": gain(ks[20], (D,)),
    }


def reference(x, positions, norm_ffn1, ffn1_up, ffn1_down, norm_mix, w_in, b_gate,
              pool_maps, pool_scale, w_pool_proj, q_latent_norm, w_uq, kv_latent_norm,
              w_ukv, w_attn_proj, w_out, norm_ffn2, ffn2_up, ffn2_down, final_norm):
    B, S, D = x.shape
    cos, sin = rope_tables(positions)
    for l in range(DEPTH):
        x = x + 0.5 * swiglu_ffn(rms_norm(x, norm_ffn1[l]), ffn1_up[l], ffn1_down[l])
        h = rms_norm(x, norm_mix[l])
        proj = h @ w_in[l]
        x_pool, q_lat, kv_lat, k_rope, gate_logits = jnp.split(proj, SPLIT_POINTS, axis=-1)
        gates = jax.nn.sigmoid((gate_logits + b_gate[l]).astype(jnp.float32)).astype(x.dtype)
        gates = gates.reshape(B, S, N_BRANCHES, D)
        branch_a = causal_multiscale_pool(x_pool, pool_maps[l], pool_scale[l]) @ w_pool_proj[l]
        branch_b = mla_attention(q_lat, kv_lat, k_rope, cos, sin, q_latent_norm[l], w_uq[l],
                                 kv_latent_norm[l], w_ukv[l]) @ w_attn_proj[l]
        merged = gates[:, :, 0] * branch_a + gates[:, :, 1] * branch_b
        x = x + merged @ w_out[l]
        x = x + 0.5 * swiglu_ffn(rms_norm(x, norm_ffn2[l]), ffn2_up[l], ffn2_down[l])
    return rms_norm(x, final_norm)
```

```python
import functools
import math

import jax
import jax.numpy as jnp
from jax import lax
from jax.experimental import pallas as pl
from jax.experimental.pallas import tpu as pltpu

F32 = jnp.float32
BF16 = jnp.bfloat16

N_HEADS = 8
NOPE = 128
ROPE = 64
HALF = ROPE // 2
V_DIM = 128
QK_DIM = NOPE + ROPE
HEAD_BLOCK = 256
ROPE_THETA = 10000.0
POOL_WINDOWS = (2, 4, 8, 16)
POOL_GROUP = 128
POOL_DIM = len(POOL_WINDOWS) * POOL_GROUP
POOL_CARRY = 32
NORM_EPS = 1e-6
LANES = 128
NEG_BIG = -1e30

FFN_TOKEN_TILE = 1024
FFN_FF_TILE = 256
MIX_TOKEN_TILE = 512
MERGE_TOKEN_TILE = 512
ATTN_TILE = 512
VMEM_LIMIT = 56 * 1024 * 1024


def _rms(x, g):
    ms = jnp.mean(x * x, axis=-1, keepdims=True)
    return x * lax.rsqrt(ms + NORM_EPS) * g


def _dot(a, b):
    return jnp.dot(a, b, preferred_element_type=F32)


def _params(sem):
    return pltpu.CompilerParams(dimension_semantics=sem, vmem_limit_bytes=VMEM_LIMIT)


def _rope_table_kernel(pos_ref, invf_ref, cos_ref, sin_ref):
    ang = pos_ref[...].astype(F32) * invf_ref[...]
    cos_ref[...] = jnp.cos(ang)
    sin_ref[...] = jnp.sin(ang)


def _rope_tables(positions):
    t = positions.size
    inv_freq = ROPE_THETA ** (-jnp.arange(0, ROPE, 2, dtype=F32) / ROPE)
    per_row = LANES // HALF
    rows = t // per_row
    pos_rep = jnp.repeat(positions.reshape(t), HALF).reshape(rows, LANES)
    invf = jnp.tile(inv_freq, per_row).reshape(1, LANES)
    tr = min(rows, 1024)
    cos, sin = pl.pallas_call(
        _rope_table_kernel,
        grid=(rows // tr,),
        in_specs=[pl.BlockSpec((tr, LANES), lambda i: (i, 0)),
                  pl.BlockSpec((1, LANES), lambda i: (0, 0))],
        out_specs=[pl.BlockSpec((tr, LANES), lambda i: (i, 0))] * 2,
        out_shape=[jax.ShapeDtypeStruct((rows, LANES), F32)] * 2,
        compiler_params=_params(("parallel",)),
        name="rope_tables",
    )(pos_rep, invf)
    cos = jnp.tile(cos.reshape(t, HALF), (1, per_row))
    sin = jnp.tile(sin.reshape(t, HALF), (1, per_row))
    return cos, sin


def _ffn_kernel(*refs, n_ff, final):
    if final:
        x_ref, g_ref, wg_ref, wu_ref, wd_ref, gf_ref, o_ref, h_ref, acc_ref = refs
    else:
        x_ref, g_ref, wg_ref, wu_ref, wd_ref, o_ref, h_ref, acc_ref = refs
    j = pl.program_id(1)

    @pl.when(j == 0)
    def _():
        h_ref[...] = _rms(x_ref[...], g_ref[...]).astype(BF16)
        acc_ref[...] = jnp.zeros_like(acc_ref)

    h = h_ref[...]
    gate = _dot(h, wg_ref[...])
    up = _dot(h, wu_ref[...])
    act = (gate * jax.nn.sigmoid(gate) * up).astype(BF16)
    acc_ref[...] += _dot(act, wd_ref[...])

    @pl.when(j == n_ff - 1)
    def _():
        y = x_ref[...] + 0.5 * acc_ref[...]
        if final:
            y = _rms(y, gf_ref[...])
        o_ref[...] = y


def _ffn(x, norm, w_up, w_down, layer, final_norm=None):
    t, d = x.shape
    f = w_down.shape[1]
    tm = min(FFN_TOKEN_TILE, t)
    tf = FFN_FF_TILE
    n_ff = f // tf
    final = final_norm is not None
    in_specs = [
        pl.BlockSpec((tm, d), lambda i, j: (i, 0)),
        pl.BlockSpec((None, 1, d), lambda i, j: (layer, 0, 0)),
        pl.BlockSpec((None, d, tf), lambda i, j: (layer, 0, j)),
        pl.BlockSpec((None, d, tf), lambda i, j: (layer, 0, j + n_ff)),
        pl.BlockSpec((None, tf, d), lambda i, j: (layer, j, 0)),
    ]
    args = [x, norm, w_up, w_up, w_down]
    if final:
        in_specs.append(pl.BlockSpec((1, d), lambda i, j: (0, 0)))
        args.append(final_norm)
    return pl.pallas_call(
        functools.partial(_ffn_kernel, n_ff=n_ff, final=final),
        grid=(t // tm, n_ff),
        in_specs=in_specs,
        out_specs=pl.BlockSpec((tm, d), lambda i, j: (i, 0)),
        out_shape=jax.ShapeDtypeStruct((t, d), F32),
        scratch_shapes=[pltpu.VMEM((tm, d), BF16), pltpu.VMEM((tm, d), F32)],
        compiler_params=_params(("parallel", "arbitrary")),
        name="ffn_final" if final else "ffn",
    )(*args)


def _mix_in_kernel(x_ref, g_ref, ws_ref, qn_ref, wq_ref, kvn_ref, wkv_ref, pm_ref, ps_ref, cos_ref, sin_ref,
                   pool_ref, q_ref, k_ref, v_ref, e_ref, t1_ref, t2_ref, t3_ref, *, tm, tiles_per_seq):
    c = POOL_CARRY
    g = POOL_GROUP
    seq_tile = lax.rem(pl.program_id(0), tiles_per_seq)

    h = _rms(x_ref[...], g_ref[...]).astype(BF16)
    p = _dot(h, ws_ref[...])
    xp = p[:, :POOL_DIM]
    q_lat = p[:, POOL_DIM:POOL_DIM + 384]
    kv_lat = p[:, POOL_DIM + 384:POOL_DIM + 640]
    kr_main = p[:, POOL_DIM + 640:POOL_DIM + 768]
    kr_rot = p[:, POOL_DIM + 768:POOL_DIM + 896]

    @pl.when(seq_tile == 0)
    def _():
        e_ref[0:c, :] = jnp.zeros((c, POOL_DIM), F32)

    e_ref[c:c + tm, :] = xp
    t1_ref[8:c + tm, :] = e_ref[8:c + tm, :] + e_ref[7:c + tm - 1, :]
    t2_ref[16:c + tm, :] = t1_ref[16:c + tm, g:4 * g] + t1_ref[14:c + tm - 2, g:4 * g]
    t3_ref[24:c + tm, :] = t2_ref[24:c + tm, g:3 * g] + t2_ref[20:c + tm - 4, g:3 * g]
    w16 = t3_ref[c:c + tm, g:2 * g] + t3_ref[c - 8:c + tm - 8, g:2 * g]
    sums = (t1_ref[c:c + tm, 0:g], t2_ref[c:c + tm, 0:g], t3_ref[c:c + tm, 0:g], w16)
    e_ref[0:c, :] = e_ref[tm:tm + c, :]

    pos1 = (seq_tile * tm + 1 + lax.broadcasted_iota(jnp.int32, (tm, 1), 0)).astype(F32)
    for gi, w in enumerate(POOL_WINDOWS):
        count = jnp.minimum(pos1, float(w))
        pooled = sums[gi] / count - xp[:, gi * g:(gi + 1) * g]
        mixed = _dot(pooled.astype(BF16), pm_ref[gi]) * ps_ref[:, gi * g:(gi + 1) * g]
        pool_ref[:, gi * g:(gi + 1) * g] = mixed.astype(BF16)

    cos = cos_ref[...]
    sin = sin_ref[...]
    lane = lax.broadcasted_iota(jnp.int32, (tm, LANES), 1)

    qa = _dot(_rms(q_lat, qn_ref[...]).astype(BF16), wq_ref[...])
    n_all = N_HEADS * NOPE
    n_pair = N_HEADS * ROPE
    for pair in range(N_HEADS // 2):
        main = qa[:, n_all + pair * LANES:n_all + (pair + 1) * LANES]
        rot = qa[:, n_all + n_pair + pair * LANES:n_all + n_pair + (pair + 1) * LANES]
        roped = main * cos + rot * sin
        for sub in range(2):
            head = 2 * pair + sub
            keep = (lane < ROPE) if sub == 0 else (lane >= ROPE)
            base = head * HEAD_BLOCK
            q_ref[:, base:base + NOPE] = qa[:, head * NOPE:(head + 1) * NOPE].astype(BF16)
            q_ref[:, base + NOPE:base + HEAD_BLOCK] = jnp.where(keep, roped, 0.0).astype(BF16)

    kr = (kr_main * cos + kr_rot * sin).astype(BF16)
    kv = _dot(_rms(kv_lat, kvn_ref[...]).astype(BF16), wkv_ref[...])
    for head in range(N_HEADS):
        base = head * HEAD_BLOCK
        k_ref[:, base:base + NOPE] = kv[:, head * NOPE:(head + 1) * NOPE].astype(BF16)
        k_ref[:, base + NOPE:base + HEAD_BLOCK] = kr
    v_ref[...] = kv[:, n_all:].astype(BF16)


def _mix_in(x, seq, norm, w_small, qn, wq, kvn, wkv, pmaps, pscale, cos, sin, layer):
    t, d = x.shape
    tm = min(MIX_TOKEN_TILE, seq)
    tiles_per_seq = seq // tm
    c = POOL_CARRY
    lay3 = lambda i: (layer, 0, 0)
    row = lambda i: (i, 0)
    in_specs = [
        pl.BlockSpec((tm, d), row),
        pl.BlockSpec((None, 1, d), lay3),
        pl.BlockSpec((None,) + w_small.shape[1:], lay3),
        pl.BlockSpec((None, 1, qn.shape[2]), lay3),
        pl.BlockSpec((None,) + wq.shape[1:], lay3),
        pl.BlockSpec((None, 1, kvn.shape[2]), lay3),
        pl.BlockSpec((None,) + wkv.shape[1:], lay3),
        pl.BlockSpec((None,) + pmaps.shape[1:], lambda i: (layer, 0, 0, 0)),
        pl.BlockSpec((None, 1, POOL_DIM), lay3),
        pl.BlockSpec((tm, LANES), row),
        pl.BlockSpec((tm, LANES), row),
    ]
    qk_w = N_HEADS * HEAD_BLOCK
    out_shape = [jax.ShapeDtypeStruct((t, POOL_DIM), BF16), jax.ShapeDtypeStruct((t, qk_w), BF16),
                 jax.ShapeDtypeStruct((t, qk_w), BF16), jax.ShapeDtypeStruct((t, N_HEADS * V_DIM), BF16)]
    out_specs = [pl.BlockSpec((tm, POOL_DIM), row), pl.BlockSpec((tm, qk_w), row),
                 pl.BlockSpec((tm, qk_w), row), pl.BlockSpec((tm, N_HEADS * V_DIM), row)]
    g = POOL_GROUP
    return pl.pallas_call(
        functools.partial(_mix_in_kernel, tm=tm, tiles_per_seq=tiles_per_seq),
        grid=(t // tm,),
        in_specs=in_specs,
        out_specs=out_specs,
        out_shape=out_shape,
        scratch_shapes=[pltpu.VMEM((c + tm, 4 * g), F32), pltpu.VMEM((c + tm, 4 * g), F32),
                        pltpu.VMEM((c + tm, 3 * g), F32), pltpu.VMEM((c + tm, 2 * g), F32)],
        compiler_params=_params(("arbitrary",)),
        name="mix_in",
    )(x, norm, w_small, qn, wq, kvn, wkv, pmaps, pscale, cos, sin)


def _attn_kernel(q_ref, k_ref, v_ref, o_ref, *, seq, tile, exp2_scale):
    n = seq // tile
    row = lax.broadcasted_iota(jnp.int32, (tile, tile), 0)
    col = lax.broadcasted_iota(jnp.int32, (tile, tile), 1)
    causal = row >= col
    for qi in range(n):
        q = q_ref[qi * tile:(qi + 1) * tile, :]
        m = l = acc = None
        for kj in range(qi + 1):
            k = k_ref[kj * tile:(kj + 1) * tile, :]
            v = v_ref[kj * tile:(kj + 1) * tile, :]
            s = lax.dot_general(q, k, (((1,), (1,)), ((), ())), preferred_element_type=F32)
            if kj == qi:
                s = jnp.where(causal, s, NEG_BIG)
            s_max = jnp.max(s, axis=-1, keepdims=True)
            if kj == 0:
                m = s_max
                p = jnp.exp2((s - m) * exp2_scale)
                l = jnp.sum(p, axis=-1, keepdims=True)
                acc = _dot(p.astype(BF16), v)
            else:
                m_new = jnp.maximum(m, s_max)
                alpha = jnp.exp2((m - m_new) * exp2_scale)
                p = jnp.exp2((s - m_new) * exp2_scale)
                l = alpha * l + jnp.sum(p, axis=-1, keepdims=True)
                acc = alpha * acc + _dot(p.astype(BF16), v)
                m = m_new
        o_ref[qi * tile:(qi + 1) * tile, :] = (acc / l).astype(BF16)


def _attention(q, k, v, batch, seq):
    t = q.shape[0]
    tile = min(ATTN_TILE, seq)
    exp2_scale = (QK_DIM ** -0.5) * math.log2(math.e)
    return pl.pallas_call(
        functools.partial(_attn_kernel, seq=seq, tile=tile, exp2_scale=exp2_scale),
        grid=(batch, N_HEADS),
        in_specs=[pl.BlockSpec((seq, HEAD_BLOCK), lambda b, h: (b, h)),
                  pl.BlockSpec((seq, HEAD_BLOCK), lambda b, h: (b, h)),
                  pl.BlockSpec((seq, V_DIM), lambda b, h: (b, h))],
        out_specs=pl.BlockSpec((seq, V_DIM), lambda b, h: (b, h)),
        out_shape=jax.ShapeDtypeStruct((t, N_HEADS * V_DIM), BF16),
        compiler_params=_params(("parallel", "parallel")),
        name="attn",
    )(q, k, v)


def _merge_kernel(x_ref, g_ref, wg_ref, bg_ref, pool_ref, wpp_ref, attn_ref, wap_ref, wout_ref, o_ref):
    x = x_ref[...]
    d = x.shape[1]
    h = _rms(x, g_ref[...]).astype(BF16)
    gates = jax.nn.sigmoid(_dot(h, wg_ref[...]) + bg_ref[...])
    branch_a = _dot(pool_ref[...], wpp_ref[...])
    branch_b = _dot(attn_ref[...], wap_ref[...])
    merged = gates[:, :d] * branch_a + gates[:, d:] * branch_b
    o_ref[...] = x + _dot(merged.astype(BF16), wout_ref[...])


def _merge(x, norm, w_gate, b_gate, pool, wpp, attn, wap, wout, layer):
    t, d = x.shape
    tm = min(MERGE_TOKEN_TILE, t)
    lay3 = lambda i: (layer, 0, 0)
    row = lambda i: (i, 0)
    return pl.pallas_call(
        _merge_kernel,
        grid=(t // tm,),
        in_specs=[
            pl.BlockSpec((tm, d), row),
            pl.BlockSpec((None, 1, d), lay3),
            pl.BlockSpec((None,) + w_gate.shape[1:], lay3),
            pl.BlockSpec((None, 1, 2 * d), lay3),
            pl.BlockSpec((tm, pool.shape[1]), row),
            pl.BlockSpec((None,) + wpp.shape[1:], lay3),
            pl.BlockSpec((tm, attn.shape[1]), row),
            pl.BlockSpec((None,) + wap.shape[1:], lay3),
            pl.BlockSpec((None,) + wout.shape[1:], lay3),
        ],
        out_specs=pl.BlockSpec((tm, d), row),
        out_shape=jax.ShapeDtypeStruct((t, d), F32),
        compiler_params=_params(("parallel",)),
        name="merge",
    )(x, norm, w_gate, b_gate, pool, wpp, attn, wap, wout)


def _rot_cols(w):
    return jnp.concatenate([-w[..., HALF:], w[..., :HALF]], axis=-1)


def _prep_w_in(w_in):
    n_lat = POOL_DIM + 384 + 256
    kr = w_in[:, :, n_lat:n_lat + ROPE]
    kr_rot = _rot_cols(kr)
    small = jnp.concatenate([w_in[:, :, :n_lat], kr, kr, kr_rot, kr_rot], axis=-1)
    return small.astype(BF16), w_in[:, :, n_lat + ROPE:].astype(BF16)


def _prep_w_uq(w_uq):
    l, r, _ = w_uq.shape
    w = w_uq.reshape(l, r, N_HEADS, QK_DIM)
    nope = w[..., :NOPE].reshape(l, r, N_HEADS * NOPE)
    rope = w[..., NOPE:]
    return jnp.concatenate([nope, rope.reshape(l, r, N_HEADS * ROPE),
                            _rot_cols(rope).reshape(l, r, N_HEADS * ROPE)], axis=-1).astype(BF16)


def _prep_w_ukv(w_ukv):
    l, r, _ = w_ukv.shape
    w = w_ukv.reshape(l, r, N_HEADS, 2, NOPE)
    return jnp.swapaxes(w, 2, 3).reshape(l, r, 2 * N_HEADS * NOPE).astype(BF16)


def kernel(x, positions, norm_ffn1, ffn1_up, ffn1_down, norm_mix, w_in, b_gate, pool_maps, pool_scale, w_pool_proj,
           q_latent_norm, w_uq, kv_latent_norm, w_ukv, w_attn_proj, w_out, norm_ffn2, ffn2_up, ffn2_down, final_norm):
    batch, seq, d = x.shape
    depth = norm_ffn1.shape[0]
    t = batch * seq
    assert seq % 128 == 0 and d == 1024

    row3 = lambda a: a.reshape(a.shape[0], 1, a.shape[1])
    w_small, w_gate = _prep_w_in(w_in)
    wq = _prep_w_uq(w_uq)
    wkv = _prep_w_ukv(w_ukv)
    up1, down1 = ffn1_up.astype(BF16), ffn1_down.astype(BF16)
    up2, down2 = ffn2_up.astype(BF16), ffn2_down.astype(BF16)
    pmaps = pool_maps.astype(BF16)
    wpp, wap, wout = w_pool_proj.astype(BF16), w_attn_proj.astype(BF16), w_out.astype(BF16)
    n1, nm, n2 = row3(norm_ffn1), row3(norm_mix), row3(norm_ffn2)
    qn, kvn, ps, bg = row3(q_latent_norm), row3(kv_latent_norm), row3(pool_scale), row3(b_gate)
    gf = final_norm.reshape(1, d)

    cos, sin = _rope_tables(positions)
    xt = x.reshape(t, d)
    for layer in range(depth):
        xt = _ffn(xt, n1, up1, down1, layer)
        pool, q, k, v = _mix_in(xt, seq, nm, w_small, qn, wq, kvn, wkv, pmaps, ps, cos, sin, layer)
        attn = _attention(q, k, v, batch, seq)
        xt = _merge(xt, nm, w_gate, bg, pool, wpp, attn, wap, wout, layer)
        xt = _ffn(xt, n2, up2, down2, layer, final_norm=gf if layer == depth - 1 else None)
    return xt.reshape(batch, seq, d)
```

```python
import functools
import math

import jax
import jax.numpy as jnp
from jax import lax
from jax.experimental import pallas as pl
from jax.experimental.pallas import tpu as pltpu

F32 = jnp.float32
BF16 = jnp.bfloat16

N_HEADS = 8
NOPE = 128
ROPE = 64
HALF = ROPE // 2
V_DIM = 128
QK_DIM = NOPE + ROPE
HEAD_BLOCK = 256
ROPE_THETA = 10000.0
POOL_WINDOWS = (2, 4, 8, 16)
POOL_GROUP = 128
POOL_DIM = len(POOL_WINDOWS) * POOL_GROUP
POOL_CARRY = 32
NORM_EPS = 1e-6
LANES = 128
NEG_BIG = -1e30

FFN_TOKEN_TILE = 512
FFN_FF_TILE = 256
MIX_TOKEN_TILE = 512
MIX_SUB_TILES = 2
MERGE_TOKEN_TILE = 512
ATTN_TILE = 512
VMEM_LIMIT = 56 * 1024 * 1024


def _rms(x, g):
    ms = jnp.mean(x * x, axis=-1, keepdims=True)
    return x * lax.rsqrt(ms + NORM_EPS) * g


def _dot(a, b):
    return jnp.dot(a, b, preferred_element_type=F32)


def _params(sem):
    return pltpu.CompilerParams(dimension_semantics=sem, vmem_limit_bytes=VMEM_LIMIT)


def _rope_table_kernel(pos_ref, invf_ref, cos_ref, sin_ref):
    ang = pos_ref[...].astype(F32) * invf_ref[...]
    cos_ref[...] = jnp.cos(ang)
    sin_ref[...] = jnp.sin(ang)


def _rope_tables(positions):
    t = positions.size
    inv_freq = ROPE_THETA ** (-jnp.arange(0, ROPE, 2, dtype=F32) / ROPE)
    per_row = LANES // HALF
    rows = t // per_row
    pos_rep = jnp.repeat(positions.reshape(t), HALF).reshape(rows, LANES)
    invf = jnp.tile(inv_freq, per_row).reshape(1, LANES)
    tr = min(rows, 1024)
    cos, sin = pl.pallas_call(
        _rope_table_kernel,
        grid=(rows // tr,),
        in_specs=[pl.BlockSpec((tr, LANES), lambda i: (i, 0)),
                  pl.BlockSpec((1, LANES), lambda i: (0, 0))],
        out_specs=[pl.BlockSpec((tr, LANES), lambda i: (i, 0))] * 2,
        out_shape=[jax.ShapeDtypeStruct((rows, LANES), F32)] * 2,
        compiler_params=_params(("parallel",)),
        name="rope_tables",
    )(pos_rep, invf)
    cos = jnp.tile(cos.reshape(t, HALF), (1, per_row))
    sin = jnp.tile(sin.reshape(t, HALF), (1, per_row))
    return cos, sin


def _ffn_kernel(*refs, d_ff, tf, final):
    if final:
        x_ref, g_ref, wup_ref, wd_ref, gf_ref, o_ref = refs
    else:
        x_ref, g_ref, wup_ref, wd_ref, o_ref = refs
    x = x_ref[...]
    h = _rms(x, g_ref[...]).astype(BF16)
    acc = None
    for lo in range(0, d_ff, tf):
        gate = _dot(h, wup_ref[:, lo:lo + tf])
        up = _dot(h, wup_ref[:, d_ff + lo:d_ff + lo + tf])
        act = (gate * jax.nn.sigmoid(gate) * up).astype(BF16)
        part = _dot(act, wd_ref[lo:lo + tf, :])
        acc = part if acc is None else acc + part
    y = x + 0.5 * acc
    if final:
        y = _rms(y, gf_ref[...])
    o_ref[...] = y


def _ffn(x, norm, w_up, w_down, layer, final_norm=None):
    t, d = x.shape
    f = w_down.shape[1]
    tm = min(FFN_TOKEN_TILE, t)
    final = final_norm is not None
    resident = pl.Buffered(1)
    in_specs = [
        pl.BlockSpec((tm, d), lambda i: (i, 0)),
        pl.BlockSpec((None, 1, d), lambda i: (layer, 0, 0)),
        pl.BlockSpec((None, d, 2 * f), lambda i: (layer, 0, 0), pipeline_mode=resident),
        pl.BlockSpec((None, f, d), lambda i: (layer, 0, 0), pipeline_mode=resident),
    ]
    args = [x, norm, w_up, w_down]
    if final:
        in_specs.append(pl.BlockSpec((1, d), lambda i: (0, 0)))
        args.append(final_norm)
    return pl.pallas_call(
        functools.partial(_ffn_kernel, d_ff=f, tf=FFN_FF_TILE, final=final),
        grid=(t // tm,),
        in_specs=in_specs,
        out_specs=pl.BlockSpec((tm, d), lambda i: (i, 0)),
        out_shape=jax.ShapeDtypeStruct((t, d), F32),
        compiler_params=_params(("parallel",)),
        name="ffn_final" if final else "ffn",
    )(*args)


def _mix_in_kernel(x_ref, g_ref, ws_ref, qn_ref, wq_ref, kvn_ref, wkv_ref, pm_ref, ps_ref, cos_ref, sin_ref,
                   pool_ref, q_ref, k_ref, v_ref, e_ref, t1_ref, t2_ref, t3_ref, *, tm, tiles_per_seq):
    c = POOL_CARRY
    g = POOL_GROUP
    seq_tile = lax.rem(pl.program_id(0), tiles_per_seq)

    @pl.when(seq_tile == 0)
    def _():
        e_ref[0:c, :] = jnp.zeros((c, POOL_DIM), F32)

    ts = tm // MIX_SUB_TILES
    lane = lax.broadcasted_iota(jnp.int32, (ts, LANES), 1)
    n_all = N_HEADS * NOPE
    n_pair = N_HEADS * ROPE
    for r0 in range(0, tm, ts):
        rows = slice(r0, r0 + ts)
        h = _rms(x_ref[rows, :], g_ref[...]).astype(BF16)
        p = _dot(h, ws_ref[...])
        e_ref[c + r0:c + r0 + ts, :] = p[:, :POOL_DIM]
        q_lat = p[:, POOL_DIM:POOL_DIM + 384]
        kv_lat = p[:, POOL_DIM + 384:POOL_DIM + 640]
        kr_main = p[:, POOL_DIM + 640:POOL_DIM + 768]
        kr_rot = p[:, POOL_DIM + 768:POOL_DIM + 896]
        cos = cos_ref[rows, :]
        sin = sin_ref[rows, :]

        qa = _dot(_rms(q_lat, qn_ref[...]).astype(BF16), wq_ref[...])
        for pair in range(N_HEADS // 2):
            main = qa[:, n_all + pair * LANES:n_all + (pair + 1) * LANES]
            rot = qa[:, n_all + n_pair + pair * LANES:n_all + n_pair + (pair + 1) * LANES]
            roped = main * cos + rot * sin
            for sub in range(2):
                head = 2 * pair + sub
                keep = (lane < ROPE) if sub == 0 else (lane >= ROPE)
                base = head * HEAD_BLOCK
                q_ref[rows, base:base + NOPE] = qa[:, head * NOPE:(head + 1) * NOPE].astype(BF16)
                q_ref[rows, base + NOPE:base + HEAD_BLOCK] = jnp.where(keep, roped, 0.0).astype(BF16)

        kr = (kr_main * cos + kr_rot * sin).astype(BF16)
        kv = _dot(_rms(kv_lat, kvn_ref[...]).astype(BF16), wkv_ref[...])
        for head in range(N_HEADS):
            base = head * HEAD_BLOCK
            k_ref[rows, base:base + NOPE] = kv[:, head * NOPE:(head + 1) * NOPE].astype(BF16)
            k_ref[rows, base + NOPE:base + HEAD_BLOCK] = kr
        v_ref[rows, :] = kv[:, n_all:].astype(BF16)

    xp = e_ref[c:c + tm, :]
    t1_ref[8:c + tm, :] = e_ref[8:c + tm, :] + e_ref[7:c + tm - 1, :]
    t2_ref[16:c + tm, :] = t1_ref[16:c + tm, g:4 * g] + t1_ref[14:c + tm - 2, g:4 * g]
    t3_ref[24:c + tm, :] = t2_ref[24:c + tm, g:3 * g] + t2_ref[20:c + tm - 4, g:3 * g]
    w16 = t3_ref[c:c + tm, g:2 * g] + t3_ref[c - 8:c + tm - 8, g:2 * g]
    sums = (t1_ref[c:c + tm, 0:g], t2_ref[c:c + tm, 0:g], t3_ref[c:c + tm, 0:g], w16)
    e_ref[0:c, :] = e_ref[tm:tm + c, :]

    pos1 = (seq_tile * tm + 1 + lax.broadcasted_iota(jnp.int32, (tm, 1), 0)).astype(F32)
    for gi, w in enumerate(POOL_WINDOWS):
        count = jnp.minimum(pos1, float(w))
        pooled = sums[gi] / count - xp[:, gi * g:(gi + 1) * g]
        mixed = _dot(pooled.astype(BF16), pm_ref[gi]) * ps_ref[:, gi * g:(gi + 1) * g]
        pool_ref[:, gi * g:(gi + 1) * g] = mixed.astype(BF16)


def _mix_in(x, seq, norm, w_small, qn, wq, kvn, wkv, pmaps, pscale, cos, sin, layer):
    t, d = x.shape
    tm = min(MIX_TOKEN_TILE, seq)
    tiles_per_seq = seq // tm
    c = POOL_CARRY
    lay3 = lambda i: (layer, 0, 0)
    row = lambda i: (i, 0)
    in_specs = [
        pl.BlockSpec((tm, d), row),
        pl.BlockSpec((None, 1, d), lay3),
        pl.BlockSpec((None,) + w_small.shape[1:], lay3),
        pl.BlockSpec((None, 1, qn.shape[2]), lay3),
        pl.BlockSpec((None,) + wq.shape[1:], lay3),
        pl.BlockSpec((None, 1, kvn.shape[2]), lay3),
        pl.BlockSpec((None,) + wkv.shape[1:], lay3),
        pl.BlockSpec((None,) + pmaps.shape[1:], lambda i: (layer, 0, 0, 0)),
        pl.BlockSpec((None, 1, POOL_DIM), lay3),
        pl.BlockSpec((tm, LANES), row),
        pl.BlockSpec((tm, LANES), row),
    ]
    qk_w = N_HEADS * HEAD_BLOCK
    out_shape = [jax.ShapeDtypeStruct((t, POOL_DIM), BF16), jax.ShapeDtypeStruct((t, qk_w), BF16),
                 jax.ShapeDtypeStruct((t, qk_w), BF16), jax.ShapeDtypeStruct((t, N_HEADS * V_DIM), BF16)]
    out_specs = [pl.BlockSpec((tm, POOL_DIM), row), pl.BlockSpec((tm, qk_w), row),
                 pl.BlockSpec((tm, qk_w), row), pl.BlockSpec((tm, N_HEADS * V_DIM), row)]
    g = POOL_GROUP
    return pl.pallas_call(
        functools.partial(_mix_in_kernel, tm=tm, tiles_per_seq=tiles_per_seq),
        grid=(t // tm,),
        in_specs=in_specs,
        out_specs=out_specs,
        out_shape=out_shape,
        scratch_shapes=[pltpu.VMEM((c + tm, 4 * g), F32), pltpu.VMEM((c + tm, 4 * g), F32),
                        pltpu.VMEM((c + tm, 3 * g), F32), pltpu.VMEM((c + tm, 2 * g), F32)],
        compiler_params=_params(("arbitrary",)),
        name="mix_in",
    )(x, norm, w_small, qn, wq, kvn, wkv, pmaps, pscale, cos, sin)


def _attn_kernel(q_ref, k_ref, v_ref, o_ref, *, seq, tile, exp2_scale):
    n = seq // tile
    row = lax.broadcasted_iota(jnp.int32, (tile, tile), 0)
    col = lax.broadcasted_iota(jnp.int32, (tile, tile), 1)
    causal = row >= col
    for qi in range(n):
        q = q_ref[qi * tile:(qi + 1) * tile, :]
        m = l = acc = None
        for kj in range(qi + 1):
            k = k_ref[kj * tile:(kj + 1) * tile, :]
            v = v_ref[kj * tile:(kj + 1) * tile, :]
            s = lax.dot_general(q, k, (((1,), (1,)), ((), ())), preferred_element_type=F32)
            if kj == qi:
                s = jnp.where(causal, s, NEG_BIG)
            s_max = jnp.max(s, axis=-1, keepdims=True)
            if kj == 0:
                m = s_max
                p = jnp.exp2((s - m) * exp2_scale)
                l = jnp.sum(p, axis=-1, keepdims=True)
                acc = _dot(p.astype(BF16), v)
            else:
                m_new = jnp.maximum(m, s_max)
                alpha = jnp.exp2((m - m_new) * exp2_scale)
                p = jnp.exp2((s - m_new) * exp2_scale)
                l = alpha * l + jnp.sum(p, axis=-1, keepdims=True)
                acc = alpha * acc + _dot(p.astype(BF16), v)
                m = m_new
        o_ref[qi * tile:(qi + 1) * tile, :] = (acc / l).astype(BF16)


def _attention(q, k, v, batch, seq):
    t = q.shape[0]
    tile = min(ATTN_TILE, seq)
    exp2_scale = (QK_DIM ** -0.5) * math.log2(math.e)
    return pl.pallas_call(
        functools.partial(_attn_kernel, seq=seq, tile=tile, exp2_scale=exp2_scale),
        grid=(batch, N_HEADS),
        in_specs=[pl.BlockSpec((seq, HEAD_BLOCK), lambda b, h: (b, h)),
                  pl.BlockSpec((seq, HEAD_BLOCK), lambda b, h: (b, h)),
                  pl.BlockSpec((seq, V_DIM), lambda b, h: (b, h))],
        out_specs=pl.BlockSpec((seq, V_DIM), lambda b, h: (b, h)),
        out_shape=jax.ShapeDtypeStruct((t, N_HEADS * V_DIM), BF16),
        compiler_params=_params(("parallel", "parallel")),
        name="attn",
    )(q, k, v)


def _merge_kernel(x_ref, g_ref, wg_ref, bg_ref, pool_ref, wpp_ref, attn_ref, wap_ref, wout_ref, o_ref):
    x = x_ref[...]
    d = x.shape[1]
    h = _rms(x, g_ref[...]).astype(BF16)
    gates = jax.nn.sigmoid(_dot(h, wg_ref[...]) + bg_ref[...])
    branch_a = _dot(pool_ref[...], wpp_ref[...])
    branch_b = _dot(attn_ref[...], wap_ref[...])
    merged = gates[:, :d] * branch_a + gates[:, d:] * branch_b
    o_ref[...] = x + _dot(merged.astype(BF16), wout_ref[...])


def _merge(x, norm, w_gate, b_gate, pool, wpp, attn, wap, wout, layer):
    t, d = x.shape
    tm = min(MERGE_TOKEN_TILE, t)
    lay3 = lambda i: (layer, 0, 0)
    row = lambda i: (i, 0)
    return pl.pallas_call(
        _merge_kernel,
        grid=(t // tm,),
        in_specs=[
            pl.BlockSpec((tm, d), row),
            pl.BlockSpec((None, 1, d), lay3),
            pl.BlockSpec((None,) + w_gate.shape[1:], lay3),
            pl.BlockSpec((None, 1, 2 * d), lay3),
            pl.BlockSpec((tm, pool.shape[1]), row),
            pl.BlockSpec((None,) + wpp.shape[1:], lay3),
            pl.BlockSpec((tm, attn.shape[1]), row),
            pl.BlockSpec((None,) + wap.shape[1:], lay3),
            pl.BlockSpec((None,) + wout.shape[1:], lay3),
        ],
        out_specs=pl.BlockSpec((tm, d), row),
        out_shape=jax.ShapeDtypeStruct((t, d), F32),
        compiler_params=_params(("parallel",)),
        name="merge",
    )(x, norm, w_gate, b_gate, pool, wpp, attn, wap, wout)


def _rot_cols(w):
    return jnp.concatenate([-w[..., HALF:], w[..., :HALF]], axis=-1)


def _prep_w_in(w_in):
    n_lat = POOL_DIM + 384 + 256
    kr = w_in[:, :, n_lat:n_lat + ROPE]
    kr_rot = _rot_cols(kr)
    small = jnp.concatenate([w_in[:, :, :n_lat], kr, kr, kr_rot, kr_rot], axis=-1)
    return small.astype(BF16), w_in[:, :, n_lat + ROPE:].astype(BF16)


def _prep_w_uq(w_uq):
    l, r, _ = w_uq.shape
    w = w_uq.reshape(l, r, N_HEADS, QK_DIM)
    nope = w[..., :NOPE].reshape(l, r, N_HEADS * NOPE)
    rope = w[..., NOPE:]
    return jnp.concatenate([nope, rope.reshape(l, r, N_HEADS * ROPE),
                            _rot_cols(rope).reshape(l, r, N_HEADS * ROPE)], axis=-1).astype(BF16)


def _prep_w_ukv(w_ukv):
    l, r, _ = w_ukv.shape
    w = w_ukv.reshape(l, r, N_HEADS, 2, NOPE)
    return jnp.swapaxes(w, 2, 3).reshape(l, r, 2 * N_HEADS * NOPE).astype(BF16)


def kernel(x, positions, norm_ffn1, ffn1_up, ffn1_down, norm_mix, w_in, b_gate, pool_maps, pool_scale, w_pool_proj,
           q_latent_norm, w_uq, kv_latent_norm, w_ukv, w_attn_proj, w_out, norm_ffn2, ffn2_up, ffn2_down, final_norm):
    batch, seq, d = x.shape
    depth = norm_ffn1.shape[0]
    t = batch * seq
    assert seq % 128 == 0 and d == 1024

    row3 = lambda a: a.reshape(a.shape[0], 1, a.shape[1])
    w_small, w_gate = _prep_w_in(w_in)
    wq = _prep_w_uq(w_uq)
    wkv = _prep_w_ukv(w_ukv)
    up1, down1 = ffn1_up.astype(BF16), ffn1_down.astype(BF16)
    up2, down2 = ffn2_up.astype(BF16), ffn2_down.astype(BF16)
    pmaps = pool_maps.astype(BF16)
    wpp, wap, wout = w_pool_proj.astype(BF16), w_attn_proj.astype(BF16), w_out.astype(BF16)
    n1, nm, n2 = row3(norm_ffn1), row3(norm_mix), row3(norm_ffn2)
    qn, kvn, ps, bg = row3(q_latent_norm), row3(kv_latent_norm), row3(pool_scale), row3(b_gate)
    gf = final_norm.reshape(1, d)

    cos, sin = _rope_tables(positions)
    xt = x.reshape(t, d)
    for layer in range(depth):
        xt = _ffn(xt, n1, up1, down1, layer)
        pool, q, k, v = _mix_in(xt, seq, nm, w_small, qn, wq, kvn, wkv, pmaps, ps, cos, sin, layer)
        attn = _attention(q, k, v, batch, seq)
        xt = _merge(xt, nm, w_gate, bg, pool, wpp, attn, wap, wout, layer)
        xt = _ffn(xt, n2, up2, down2, layer, final_norm=gf if layer == depth - 1 else None)
    return xt.reshape(batch, seq, d)
```

```python
import functools
import math

import jax
import jax.numpy as jnp
from jax import lax
from jax.experimental import pallas as pl
from jax.experimental.pallas import tpu as pltpu

F32 = jnp.float32
BF16 = jnp.bfloat16

N_HEADS = 8
NOPE = 128
ROPE = 64
HALF = ROPE // 2
V_DIM = 128
QK_DIM = NOPE + ROPE
HEAD_BLOCK = 256
ROPE_THETA = 10000.0
POOL_WINDOWS = (2, 4, 8, 16)
POOL_GROUP = 128
POOL_DIM = len(POOL_WINDOWS) * POOL_GROUP
POOL_CARRY = 32
NORM_EPS = 1e-6
LANES = 128
NEG_BIG = -1e30

FFN_TOKEN_TILE = 1024
FFN_FF_TILE = 256
MIX_TOKEN_TILE = 512
MIX_SUB_TILES = 2
MERGE_TOKEN_TILE = 512
ATTN_TILE = 512
ATTN_KEY_TILE = 512
ATTN_HEADS_PER_STEP = 2
VMEM_LIMIT = 56 * 1024 * 1024


def _rms(x, g):
    ms = jnp.mean(x * x, axis=-1, keepdims=True)
    return x * lax.rsqrt(ms + NORM_EPS) * g


def _dot(a, b):
    return jnp.dot(a, b, preferred_element_type=F32)


def _dot_t(a, b):
    return lax.dot_general(a, b, (((1,), (1,)), ((), ())), preferred_element_type=F32)


def _rotate_half(x, first_half):
    return jnp.where(first_half, -pltpu.roll(x, LANES - HALF, 1), pltpu.roll(x, HALF, 1))


def _params(sem):
    return pltpu.CompilerParams(dimension_semantics=sem, vmem_limit_bytes=VMEM_LIMIT)


def _rope_table_kernel(pos_ref, invf_ref, cos_ref, sin_ref):
    ang = pos_ref[...].astype(F32) * invf_ref[...]
    cos_ref[...] = jnp.cos(ang)
    sin_ref[...] = jnp.sin(ang)


def _rope_tables(positions):
    t = positions.size
    inv_freq = ROPE_THETA ** (-jnp.arange(0, ROPE, 2, dtype=F32) / ROPE)
    per_row = LANES // HALF
    rows = t // per_row
    pos_rep = jnp.repeat(positions.reshape(t), HALF).reshape(rows, LANES)
    invf = jnp.tile(inv_freq, per_row).reshape(1, LANES)
    tr = min(rows, 1024)
    cos, sin = pl.pallas_call(
        _rope_table_kernel,
        grid=(rows // tr,),
        in_specs=[pl.BlockSpec((tr, LANES), lambda i: (i, 0)),
                  pl.BlockSpec((1, LANES), lambda i: (0, 0))],
        out_specs=[pl.BlockSpec((tr, LANES), lambda i: (i, 0))] * 2,
        out_shape=[jax.ShapeDtypeStruct((rows, LANES), F32)] * 2,
        compiler_params=_params(("parallel",)),
        name="rope_tables",
    )(pos_rep, invf)
    cos = jnp.tile(cos.reshape(t, HALF), (1, per_row))
    sin = jnp.tile(sin.reshape(t, HALF), (1, per_row))
    return cos, sin


def _ffn_kernel(*refs, d_ff, tf, final):
    if final:
        x_ref, g_ref, wup_ref, wd_ref, gf_ref, o_ref = refs
    else:
        x_ref, g_ref, wup_ref, wd_ref, o_ref = refs
    x = x_ref[...]
    h = _rms(x, g_ref[...]).astype(BF16)
    acc = None
    for lo in range(0, d_ff, tf):
        gate = _dot(h, wup_ref[:, lo:lo + tf])
        up = _dot(h, wup_ref[:, d_ff + lo:d_ff + lo + tf])
        act = (gate * jax.nn.sigmoid(gate) * up).astype(BF16)
        part = _dot(act, wd_ref[lo:lo + tf, :])
        acc = part if acc is None else acc + part
    y = x + 0.5 * acc
    if final:
        y = _rms(y, gf_ref[...])
    o_ref[...] = y


def _ffn(x, norm, w_up, w_down, layer, final_norm=None):
    t, d = x.shape
    f = w_down.shape[1]
    tm = min(FFN_TOKEN_TILE, t)
    final = final_norm is not None
    resident = pl.Buffered(1)
    in_specs = [
        pl.BlockSpec((tm, d), lambda i: (i, 0)),
        pl.BlockSpec((None, 1, d), lambda i: (layer, 0, 0)),
        pl.BlockSpec((None, d, 2 * f), lambda i: (layer, 0, 0), pipeline_mode=resident),
        pl.BlockSpec((None, f, d), lambda i: (layer, 0, 0), pipeline_mode=resident),
    ]
    args = [x, norm, w_up, w_down]
    if final:
        in_specs.append(pl.BlockSpec((1, d), lambda i: (0, 0)))
        args.append(final_norm)
    return pl.pallas_call(
        functools.partial(_ffn_kernel, d_ff=f, tf=FFN_FF_TILE, final=final),
        grid=(t // tm,),
        in_specs=in_specs,
        out_specs=pl.BlockSpec((tm, d), lambda i: (i, 0)),
        out_shape=jax.ShapeDtypeStruct((t, d), F32),
        compiler_params=_params(("parallel",)),
        name="ffn_final" if final else "ffn",
    )(*args)


def _mix_in_kernel(x_ref, g_ref, wl_ref, wkr_ref, qn_ref, wq_ref, kvn_ref, wkv_ref, pm_ref, ps_ref, cos_ref, sin_ref,
                   pool_ref, q_ref, k_ref, v_ref, e_ref, t1_ref, t2_ref, t3_ref, *, tm, tiles_per_seq):
    c = POOL_CARRY
    g = POOL_GROUP
    seq_tile = lax.rem(pl.program_id(0), tiles_per_seq)

    @pl.when(seq_tile == 0)
    def _():
        e_ref[0:c, :] = jnp.zeros((c, POOL_DIM), F32)

    ts = tm // MIX_SUB_TILES
    lane = lax.broadcasted_iota(jnp.int32, (ts, LANES), 1)
    first_half = (lane & (ROPE - 1)) < HALF
    n_all = N_HEADS * NOPE
    for r0 in range(0, tm, ts):
        rows = slice(r0, r0 + ts)
        h = _rms(x_ref[rows, :], g_ref[...]).astype(BF16)
        p = _dot_t(h, wl_ref[...])
        kr_main = _dot_t(h, wkr_ref[...])
        e_ref[c + r0:c + r0 + ts, :] = p[:, :POOL_DIM]
        q_lat = p[:, POOL_DIM:POOL_DIM + 384]
        kv_lat = p[:, POOL_DIM + 384:POOL_DIM + 640]
        cos = cos_ref[rows, :]
        sin = sin_ref[rows, :]

        qa = _dot(_rms(q_lat, qn_ref[...]).astype(BF16), wq_ref[...])
        for pair in range(N_HEADS // 2):
            main = qa[:, n_all + pair * LANES:n_all + (pair + 1) * LANES]
            roped = main * cos + _rotate_half(main, first_half) * sin
            for sub in range(2):
                head = 2 * pair + sub
                keep = (lane < ROPE) if sub == 0 else (lane >= ROPE)
                base = head * HEAD_BLOCK
                q_ref[rows, base:base + NOPE] = qa[:, head * NOPE:(head + 1) * NOPE].astype(BF16)
                q_ref[rows, base + NOPE:base + HEAD_BLOCK] = jnp.where(keep, roped, 0.0).astype(BF16)

        kr = (kr_main * cos + _rotate_half(kr_main, first_half) * sin).astype(BF16)
        kv = _dot(_rms(kv_lat, kvn_ref[...]).astype(BF16), wkv_ref[...])
        for head in range(N_HEADS):
            base = head * HEAD_BLOCK
            k_ref[rows, base:base + NOPE] = kv[:, head * NOPE:(head + 1) * NOPE].astype(BF16)
            k_ref[rows, base + NOPE:base + HEAD_BLOCK] = kr
        v_ref[rows, :] = kv[:, n_all:].astype(BF16)

    xp = e_ref[c:c + tm, :]
    t1_ref[8:c + tm, :] = e_ref[8:c + tm, :] + e_ref[7:c + tm - 1, :]
    t2_ref[16:c + tm, :] = t1_ref[16:c + tm, g:4 * g] + t1_ref[14:c + tm - 2, g:4 * g]
    t3_ref[24:c + tm, :] = t2_ref[24:c + tm, g:3 * g] + t2_ref[20:c + tm - 4, g:3 * g]
    w16 = t3_ref[c:c + tm, g:2 * g] + t3_ref[c - 8:c + tm - 8, g:2 * g]
    sums = (t1_ref[c:c + tm, 0:g], t2_ref[c:c + tm, 0:g], t3_ref[c:c + tm, 0:g], w16)
    e_ref[0:c, :] = e_ref[tm:tm + c, :]

    pos1 = (seq_tile * tm + 1 + lax.broadcasted_iota(jnp.int32, (tm, 1), 0)).astype(F32)
    for gi, w in enumerate(POOL_WINDOWS):
        count = jnp.minimum(pos1, float(w))
        pooled = sums[gi] / count - xp[:, gi * g:(gi + 1) * g]
        mixed = _dot(pooled.astype(BF16), pm_ref[gi]) * ps_ref[:, gi * g:(gi + 1) * g]
        pool_ref[:, gi * g:(gi + 1) * g] = mixed.astype(BF16)


def _mix_in(x, seq, norm, w_lat, w_kr, qn, wq, kvn, wkv, pmaps, pscale, cos, sin, layer):
    t, d = x.shape
    tm = min(MIX_TOKEN_TILE, seq)
    tiles_per_seq = seq // tm
    c = POOL_CARRY
    lay3 = lambda i: (layer, 0, 0)
    row = lambda i: (i, 0)
    in_specs = [
        pl.BlockSpec((tm, d), row),
        pl.BlockSpec((None, 1, d), lay3),
        pl.BlockSpec((None,) + w_lat.shape[1:], lay3),
        pl.BlockSpec((None,) + w_kr.shape[1:], lay3),
        pl.BlockSpec((None, 1, qn.shape[2]), lay3),
        pl.BlockSpec((None,) + wq.shape[1:], lay3),
        pl.BlockSpec((None, 1, kvn.shape[2]), lay3),
        pl.BlockSpec((None,) + wkv.shape[1:], lay3),
        pl.BlockSpec((None,) + pmaps.shape[1:], lambda i: (layer, 0, 0, 0)),
        pl.BlockSpec((None, 1, POOL_DIM), lay3),
        pl.BlockSpec((tm, LANES), row),
        pl.BlockSpec((tm, LANES), row),
    ]
    qk_w = N_HEADS * HEAD_BLOCK
    out_shape = [jax.ShapeDtypeStruct((t, POOL_DIM), BF16), jax.ShapeDtypeStruct((t, qk_w), BF16),
                 jax.ShapeDtypeStruct((t, qk_w), BF16), jax.ShapeDtypeStruct((t, N_HEADS * V_DIM), BF16)]
    out_specs = [pl.BlockSpec((tm, POOL_DIM), row), pl.BlockSpec((tm, qk_w), row),
                 pl.BlockSpec((tm, qk_w), row), pl.BlockSpec((tm, N_HEADS * V_DIM), row)]
    g = POOL_GROUP
    return pl.pallas_call(
        functools.partial(_mix_in_kernel, tm=tm, tiles_per_seq=tiles_per_seq),
        grid=(t // tm,),
        in_specs=in_specs,
        out_specs=out_specs,
        out_shape=out_shape,
        scratch_shapes=[pltpu.VMEM((c + tm, 4 * g), F32), pltpu.VMEM((c + tm, 4 * g), F32),
                        pltpu.VMEM((c + tm, 3 * g), F32), pltpu.VMEM((c + tm, 2 * g), F32)],
        compiler_params=_params(("arbitrary",)),
        name="mix_in",
    )(x, norm, w_lat, w_kr, qn, wq, kvn, wkv, pmaps, pscale, cos, sin)


def _attn_kernel(q_ref, k_ref, v_ref, o_ref, *, seq, tile, heads, exp2_scale):
    tq, tk = tile, ATTN_KEY_TILE
    row = lax.broadcasted_iota(jnp.int32, (tq, tk), 0)
    col = lax.broadcasted_iota(jnp.int32, (tq, tk), 1)
    for qi in range(seq // tq):
        q0 = qi * tq
        for hh in range(heads):
            qk_cols = slice(hh * HEAD_BLOCK, (hh + 1) * HEAD_BLOCK)
            v_cols = slice(hh * V_DIM, (hh + 1) * V_DIM)
            q = q_ref[q0:q0 + tq, qk_cols]
            m = l = acc = None
            for k0 in range(0, q0 + tq, tk):
                k = k_ref[k0:k0 + tk, qk_cols]
                v = v_ref[k0:k0 + tk, v_cols]
                s = _dot_t(q, k)
                if k0 + tk - 1 > q0:
                    s = jnp.where(row + q0 >= col + k0, s, NEG_BIG)
                s_max = jnp.max(s, axis=-1, keepdims=True)
                if k0 == 0:
                    m = s_max
                    p = jnp.exp2((s - m) * exp2_scale)
                    l = jnp.sum(p, axis=-1, keepdims=True)
                    acc = _dot(p.astype(BF16), v)
                else:
                    m_new = jnp.maximum(m, s_max)
                    alpha = jnp.exp2((m - m_new) * exp2_scale)
                    p = jnp.exp2((s - m_new) * exp2_scale)
                    l = alpha * l + jnp.sum(p, axis=-1, keepdims=True)
                    acc = alpha * acc + _dot(p.astype(BF16), v)
                    m = m_new
            o_ref[q0:q0 + tq, v_cols] = (acc / l).astype(BF16)


def _attention(q, k, v, batch, seq):
    t = q.shape[0]
    tile = min(ATTN_TILE, seq)
    exp2_scale = (QK_DIM ** -0.5) * math.log2(math.e)
    hp = ATTN_HEADS_PER_STEP
    return pl.pallas_call(
        functools.partial(_attn_kernel, seq=seq, tile=tile, heads=hp, exp2_scale=exp2_scale),
        grid=(batch, N_HEADS // hp),
        in_specs=[pl.BlockSpec((seq, hp * HEAD_BLOCK), lambda b, h: (b, h)),
                  pl.BlockSpec((seq, hp * HEAD_BLOCK), lambda b, h: (b, h)),
                  pl.BlockSpec((seq, hp * V_DIM), lambda b, h: (b, h))],
        out_specs=pl.BlockSpec((seq, hp * V_DIM), lambda b, h: (b, h)),
        out_shape=jax.ShapeDtypeStruct((t, N_HEADS * V_DIM), BF16),
        compiler_params=_params(("parallel", "parallel")),
        name="attn",
    )(q, k, v)


def _merge_kernel(x_ref, g_ref, wg_ref, bg_ref, pool_ref, wpp_ref, attn_ref, wap_ref, wout_ref, o_ref):
    x = x_ref[...]
    d = x.shape[1]
    h = _rms(x, g_ref[...]).astype(BF16)
    gates = jax.nn.sigmoid(_dot_t(h, wg_ref[...]) + bg_ref[...])
    branch_a = _dot(pool_ref[...], wpp_ref[...])
    branch_b = _dot(attn_ref[...], wap_ref[...])
    merged = gates[:, :d] * branch_a + gates[:, d:] * branch_b
    o_ref[...] = x + _dot(merged.astype(BF16), wout_ref[...])


def _merge(x, norm, w_gate, b_gate, pool, wpp, attn, wap, wout, layer):
    t, d = x.shape
    tm = min(MERGE_TOKEN_TILE, t)
    lay3 = lambda i: (layer, 0, 0)
    row = lambda i: (i, 0)
    return pl.pallas_call(
        _merge_kernel,
        grid=(t // tm,),
        in_specs=[
            pl.BlockSpec((tm, d), row),
            pl.BlockSpec((None, 1, d), lay3),
            pl.BlockSpec((None,) + w_gate.shape[1:], lay3),
            pl.BlockSpec((None, 1, 2 * d), lay3),
            pl.BlockSpec((tm, pool.shape[1]), row),
            pl.BlockSpec((None,) + wpp.shape[1:], lay3),
            pl.BlockSpec((tm, attn.shape[1]), row),
            pl.BlockSpec((None,) + wap.shape[1:], lay3),
            pl.BlockSpec((None,) + wout.shape[1:], lay3),
        ],
        out_specs=pl.BlockSpec((tm, d), row),
        out_shape=jax.ShapeDtypeStruct((t, d), F32),
        compiler_params=_params(("parallel",)),
        name="merge",
    )(x, norm, w_gate, b_gate, pool, wpp, attn, wap, wout)


def _prep_w_in(w_in):
    n_lat = POOL_DIM + 384 + 256
    w_t = jnp.swapaxes(w_in, 1, 2)
    kr = w_t[:, n_lat:n_lat + ROPE]
    w_kr = jnp.concatenate([kr, kr], axis=1)
    return w_t[:, :n_lat].astype(BF16), w_kr.astype(BF16), w_t[:, n_lat + ROPE:].astype(BF16)


def _prep_w_uq(w_uq):
    l, r, _ = w_uq.shape
    w = w_uq.reshape(l, r, N_HEADS, QK_DIM)
    nope = w[..., :NOPE].reshape(l, r, N_HEADS * NOPE)
    rope = w[..., NOPE:]
    return jnp.concatenate([nope, rope.reshape(l, r, N_HEADS * ROPE)], axis=-1).astype(BF16)


def _prep_w_ukv(w_ukv):
    l, r, _ = w_ukv.shape
    w = w_ukv.reshape(l, r, N_HEADS, 2, NOPE)
    return jnp.swapaxes(w, 2, 3).reshape(l, r, 2 * N_HEADS * NOPE).astype(BF16)


def kernel(x, positions, norm_ffn1, ffn1_up, ffn1_down, norm_mix, w_in, b_gate, pool_maps, pool_scale, w_pool_proj,
           q_latent_norm, w_uq, kv_latent_norm, w_ukv, w_attn_proj, w_out, norm_ffn2, ffn2_up, ffn2_down, final_norm):
    batch, seq, d = x.shape
    depth = norm_ffn1.shape[0]
    t = batch * seq
    assert seq % 128 == 0 and d == 1024

    row3 = lambda a: a.reshape(a.shape[0], 1, a.shape[1])
    w_lat, w_kr, w_gate = _prep_w_in(w_in)
    wq = _prep_w_uq(w_uq)
    wkv = _prep_w_ukv(w_ukv)
    up1, down1 = ffn1_up.astype(BF16), ffn1_down.astype(BF16)
    up2, down2 = ffn2_up.astype(BF16), ffn2_down.astype(BF16)
    pmaps = pool_maps.astype(BF16)
    wpp, wap, wout = w_pool_proj.astype(BF16), w_attn_proj.astype(BF16), w_out.astype(BF16)
    n1, nm, n2 = row3(norm_ffn1), row3(norm_mix), row3(norm_ffn2)
    qn, kvn, ps, bg = row3(q_latent_norm), row3(kv_latent_norm), row3(pool_scale), row3(b_gate)
    gf = final_norm.reshape(1, d)

    cos, sin = _rope_tables(positions)
    xt = x.reshape(t, d)
    for layer in range(depth):
        xt = _ffn(xt, n1, up1, down1, layer)
        pool, q, k, v = _mix_in(xt, seq, nm, w_lat, w_kr, qn, wq, kvn, wkv, pmaps, ps, cos, sin, layer)
        attn = _attention(q, k, v, batch, seq)
        xt = _merge(xt, nm, w_gate, bg, pool, wpp, attn, wap, wout, layer)
        xt = _ffn(xt, n2, up2, down2, layer, final_norm=gf if layer == depth - 1 else None)
    return xt.reshape(batch, seq, d)
```

```python
import functools
import math

import jax
import jax.numpy as jnp
from jax import lax
from jax.experimental import pallas as pl
from jax.experimental.pallas import tpu as pltpu

F32 = jnp.float32
BF16 = jnp.bfloat16

N_HEADS = 8
NOPE = 128
ROPE = 64
HALF = ROPE // 2
V_DIM = 128
QK_DIM = NOPE + ROPE
HEAD_BLOCK = 256
ROPE_THETA = 10000.0
POOL_WINDOWS = (2, 4, 8, 16)
POOL_GROUP = 128
POOL_DIM = len(POOL_WINDOWS) * POOL_GROUP
POOL_CARRY = 32
NORM_EPS = 1e-6
LANES = 128
NEG_BIG = -1e30

ROPE_TABLE_BLOCK = 4096
FFN_TOKEN_TILE = 1024
FFN_FF_TILE = 256
MIX_TOKEN_TILE = 1024
MIX_SUB_TILES = 4
MERGE_TOKEN_TILE = 1024
MERGE_SUB_TILES = 2
ATTN_TILE = 512
ATTN_KEY_TILE = 512
ATTN_HEADS_PER_STEP = 2
VMEM_LIMIT = 56 * 1024 * 1024


def _rms(x, g):
    ms = jnp.mean(x * x, axis=-1, keepdims=True)
    return x * lax.rsqrt(ms + NORM_EPS) * g


def _dot(a, b):
    return jnp.dot(a, b, preferred_element_type=F32)


def _dot_t(a, b):
    return lax.dot_general(a, b, (((1,), (1,)), ((), ())), preferred_element_type=F32)


def _rotate_half(x, first_half):
    return jnp.where(first_half, -pltpu.roll(x, LANES - HALF, 1), pltpu.roll(x, HALF, 1))


def _params(sem):
    return pltpu.CompilerParams(dimension_semantics=sem, vmem_limit_bytes=VMEM_LIMIT)


def _rope_table_kernel(pos_ref, invf_ref, cos_ref, sin_ref):
    rows = pos_ref.shape[0]
    groups = LANES // HALF
    ang = pos_ref[...].astype(F32) * invf_ref[...]
    group = lax.broadcasted_iota(jnp.int32, (rows, LANES), 1) // HALF
    for table, out_ref in ((jnp.cos(ang), cos_ref), (jnp.sin(ang), sin_ref)):
        rolled = [table] + [pltpu.roll(table, HALF * k, 1) for k in range(1, groups)]
        for g in range(groups):
            spread = rolled[(-g) % groups]
            for k in range(1, groups):
                spread = jnp.where(group == k, rolled[(k - g) % groups], spread)
            out_ref[g * rows:(g + 1) * rows, :] = spread


def _rope_tables(positions):
    t = positions.size
    inv_freq = ROPE_THETA ** (-jnp.arange(0, ROPE, 2, dtype=F32) / ROPE)
    groups = LANES // HALF
    block = min(t, ROPE_TABLE_BLOCK)
    rows = block // groups
    pos = positions.reshape(t // block, groups, rows)
    pos = jnp.repeat(jnp.swapaxes(pos, 1, 2), HALF, axis=2).reshape(t // groups, LANES)
    invf = jnp.tile(inv_freq, groups).reshape(1, LANES)
    return pl.pallas_call(
        _rope_table_kernel,
        grid=(t // block,),
        in_specs=[pl.BlockSpec((rows, LANES), lambda i: (i, 0)),
                  pl.BlockSpec((1, LANES), lambda i: (0, 0))],
        out_specs=[pl.BlockSpec((block, LANES), lambda i: (i, 0))] * 2,
        out_shape=[jax.ShapeDtypeStruct((t, LANES), F32)] * 2,
        compiler_params=_params(("parallel",)),
        name="rope_tables",
    )(pos, invf)


def _ffn_kernel(*refs, d_ff, tf, final):
    if final:
        x_ref, g_ref, wup_ref, wd_ref, gf_ref, o_ref = refs
    else:
        x_ref, g_ref, wup_ref, wd_ref, o_ref = refs
    x = x_ref[...]
    h = _rms(x, g_ref[...]).astype(BF16)
    acc = None
    for lo in range(0, d_ff, tf):
        gate = _dot(h, wup_ref[:, lo:lo + tf])
        up = _dot(h, wup_ref[:, d_ff + lo:d_ff + lo + tf])
        act = (gate * jax.nn.sigmoid(gate) * up).astype(BF16)
        part = _dot(act, wd_ref[lo:lo + tf, :])
        acc = part if acc is None else acc + part
    y = x + 0.5 * acc
    if final:
        y = _rms(y, gf_ref[...])
    o_ref[...] = y


def _ffn(x, norm, w_up, w_down, layer, final_norm=None):
    t, d = x.shape
    f = w_down.shape[1]
    tm = min(FFN_TOKEN_TILE, t)
    final = final_norm is not None
    resident = pl.Buffered(1)
    in_specs = [
        pl.BlockSpec((tm, d), lambda i: (i, 0)),
        pl.BlockSpec((None, 1, d), lambda i: (layer, 0, 0)),
        pl.BlockSpec((None, d, 2 * f), lambda i: (layer, 0, 0), pipeline_mode=resident),
        pl.BlockSpec((None, f, d), lambda i: (layer, 0, 0), pipeline_mode=resident),
    ]
    args = [x, norm, w_up, w_down]
    if final:
        in_specs.append(pl.BlockSpec((1, d), lambda i: (0, 0)))
        args.append(final_norm)
    return pl.pallas_call(
        functools.partial(_ffn_kernel, d_ff=f, tf=FFN_FF_TILE, final=final),
        grid=(t // tm,),
        in_specs=in_specs,
        out_specs=pl.BlockSpec((tm, d), lambda i: (i, 0)),
        out_shape=jax.ShapeDtypeStruct((t, d), F32),
        compiler_params=_params(("parallel",)),
        name="ffn_final" if final else "ffn",
    )(*args)


def _mix_in_kernel(x_ref, g_ref, wl_ref, qn_ref, wq_ref, kvn_ref, wkv_ref, pm_ref, ps_ref, cos_ref, sin_ref,
                   pool_ref, q_ref, k_ref, v_ref, e_ref, t1_ref, t2_ref, t3_ref, *, tm, tiles_per_seq):
    c = POOL_CARRY
    g = POOL_GROUP
    seq_tile = lax.rem(pl.program_id(0), tiles_per_seq)

    @pl.when(seq_tile == 0)
    def _():
        e_ref[0:c, :] = jnp.zeros((c, POOL_DIM), F32)

    ts = tm // MIX_SUB_TILES
    lane = lax.broadcasted_iota(jnp.int32, (ts, LANES), 1)
    first_half = (lane & (ROPE - 1)) < HALF
    n_all = N_HEADS * NOPE
    for r0 in range(0, tm, ts):
        rows = slice(r0, r0 + ts)
        h = _rms(x_ref[rows, :], g_ref[...]).astype(BF16)
        p = _dot_t(h, wl_ref[...])
        e_ref[c + r0:c + r0 + ts, :] = p[:, :POOL_DIM]
        q_lat = p[:, POOL_DIM:POOL_DIM + 384]
        kv_lat = p[:, POOL_DIM + 384:POOL_DIM + 640]
        kr_main = p[:, POOL_DIM + 640:POOL_DIM + 768]
        cos = cos_ref[rows, :]
        sin = sin_ref[rows, :]

        qa = _dot(_rms(q_lat, qn_ref[...]).astype(BF16), wq_ref[...])
        for pair in range(N_HEADS // 2):
            main = qa[:, n_all + pair * LANES:n_all + (pair + 1) * LANES]
            roped = main * cos + _rotate_half(main, first_half) * sin
            for sub in range(2):
                head = 2 * pair + sub
                keep = (lane < ROPE) if sub == 0 else (lane >= ROPE)
                base = head * HEAD_BLOCK
                q_ref[rows, base:base + NOPE] = qa[:, head * NOPE:(head + 1) * NOPE].astype(BF16)
                q_ref[rows, base + NOPE:base + HEAD_BLOCK] = jnp.where(keep, roped, 0.0).astype(BF16)

        kr = (kr_main * cos + _rotate_half(kr_main, first_half) * sin).astype(BF16)
        kv = _dot(_rms(kv_lat, kvn_ref[...]).astype(BF16), wkv_ref[...])
        for head in range(N_HEADS):
            base = head * HEAD_BLOCK
            k_ref[rows, base:base + NOPE] = kv[:, head * NOPE:(head + 1) * NOPE].astype(BF16)
            k_ref[rows, base + NOPE:base + HEAD_BLOCK] = kr
        v_ref[rows, :] = kv[:, n_all:].astype(BF16)

        lo, hi = c + r0, c + r0 + ts
        l1, l2, l3 = (8, 16, 24) if r0 == 0 else (lo, lo, lo)
        t1_ref[l1:hi, :] = e_ref[l1:hi, :] + e_ref[l1 - 1:hi - 1, :]
        t2_ref[l2:hi, :] = t1_ref[l2:hi, g:4 * g] + t1_ref[l2 - 2:hi - 2, g:4 * g]
        t3_ref[l3:hi, :] = t2_ref[l3:hi, g:3 * g] + t2_ref[l3 - 4:hi - 4, g:3 * g]
        w16 = t3_ref[lo:hi, g:2 * g] + t3_ref[lo - 8:hi - 8, g:2 * g]
        sums = (t1_ref[lo:hi, 0:g], t2_ref[lo:hi, 0:g], t3_ref[lo:hi, 0:g], w16)
        pos1 = (seq_tile * tm + r0 + 1 + lax.broadcasted_iota(jnp.int32, (ts, 1), 0)).astype(F32)
        pooled = [(sums[gi] / jnp.minimum(pos1, float(w)) - p[:, gi * g:(gi + 1) * g]).astype(BF16)
                  for gi, w in enumerate(POOL_WINDOWS)]
        for gi in range(len(POOL_WINDOWS)):
            cols = slice(gi * g, (gi + 1) * g)
            pool_ref[rows, cols] = (_dot(pooled[gi], pm_ref[gi]) * ps_ref[:, cols]).astype(BF16)

    e_ref[0:c, :] = e_ref[tm:tm + c, :]


def _mix_in(x, seq, norm, w_lat, qn, wq, kvn, wkv, pmaps, pscale, cos, sin, layer):
    t, d = x.shape
    tm = min(MIX_TOKEN_TILE, seq)
    tiles_per_seq = seq // tm
    c = POOL_CARRY
    lay3 = lambda i: (layer, 0, 0)
    row = lambda i: (i, 0)
    in_specs = [
        pl.BlockSpec((tm, d), row),
        pl.BlockSpec((None, 1, d), lay3),
        pl.BlockSpec((None,) + w_lat.shape[1:], lay3),
        pl.BlockSpec((None, 1, qn.shape[2]), lay3),
        pl.BlockSpec((None,) + wq.shape[1:], lay3),
        pl.BlockSpec((None, 1, kvn.shape[2]), lay3),
        pl.BlockSpec((None,) + wkv.shape[1:], lay3),
        pl.BlockSpec((None,) + pmaps.shape[1:], lambda i: (layer, 0, 0, 0)),
        pl.BlockSpec((None, 1, POOL_DIM), lay3),
        pl.BlockSpec((tm, LANES), row),
        pl.BlockSpec((tm, LANES), row),
    ]
    qk_w = N_HEADS * HEAD_BLOCK
    out_shape = [jax.ShapeDtypeStruct((t, POOL_DIM), BF16), jax.ShapeDtypeStruct((t, qk_w), BF16),
                 jax.ShapeDtypeStruct((t, qk_w), BF16), jax.ShapeDtypeStruct((t, N_HEADS * V_DIM), BF16)]
    out_specs = [pl.BlockSpec((tm, POOL_DIM), row), pl.BlockSpec((tm, qk_w), row),
                 pl.BlockSpec((tm, qk_w), row), pl.BlockSpec((tm, N_HEADS * V_DIM), row)]
    g = POOL_GROUP
    return pl.pallas_call(
        functools.partial(_mix_in_kernel, tm=tm, tiles_per_seq=tiles_per_seq),
        grid=(t // tm,),
        in_specs=in_specs,
        out_specs=out_specs,
        out_shape=out_shape,
        scratch_shapes=[pltpu.VMEM((c + tm, 4 * g), F32), pltpu.VMEM((c + tm, 4 * g), F32),
                        pltpu.VMEM((c + tm, 3 * g), F32), pltpu.VMEM((c + tm, 2 * g), F32)],
        compiler_params=_params(("arbitrary",)),
        name="mix_in",
    )(x, norm, w_lat, qn, wq, kvn, wkv, pmaps, pscale, cos, sin)


def _attn_kernel(q_ref, k_ref, v_ref, o_ref, *, seq, tile, heads, exp2_scale):
    tq, tk = tile, ATTN_KEY_TILE
    row = lax.broadcasted_iota(jnp.int32, (tq, tk), 0)
    col = lax.broadcasted_iota(jnp.int32, (tq, tk), 1)
    for qi in range(seq // tq):
        q0 = qi * tq
        for hh in range(heads):
            qk_cols = slice(hh * HEAD_BLOCK, (hh + 1) * HEAD_BLOCK)
            v_cols = slice(hh * V_DIM, (hh + 1) * V_DIM)
            q = q_ref[q0:q0 + tq, qk_cols]
            m = l = acc = None
            for k0 in range(0, q0 + tq, tk):
                k = k_ref[k0:k0 + tk, qk_cols]
                v = v_ref[k0:k0 + tk, v_cols]
                s = _dot_t(q, k)
                if k0 + tk - 1 > q0:
                    s = jnp.where(row + q0 >= col + k0, s, NEG_BIG)
                s_max = jnp.max(s, axis=-1, keepdims=True)
                if k0 == 0:
                    m = s_max
                    p = jnp.exp2((s - m) * exp2_scale)
                    l = jnp.sum(p, axis=-1, keepdims=True)
                    acc = _dot(p.astype(BF16), v)
                else:
                    m_new = jnp.maximum(m, s_max)
                    alpha = jnp.exp2((m - m_new) * exp2_scale)
                    p = jnp.exp2((s - m_new) * exp2_scale)
                    l = alpha * l + jnp.sum(p, axis=-1, keepdims=True)
                    acc = alpha * acc + _dot(p.astype(BF16), v)
                    m = m_new
            o_ref[q0:q0 + tq, v_cols] = (acc / l).astype(BF16)


def _attention(q, k, v, batch, seq):
    t = q.shape[0]
    tile = min(ATTN_TILE, seq)
    exp2_scale = (QK_DIM ** -0.5) * math.log2(math.e)
    hp = ATTN_HEADS_PER_STEP
    return pl.pallas_call(
        functools.partial(_attn_kernel, seq=seq, tile=tile, heads=hp, exp2_scale=exp2_scale),
        grid=(batch, N_HEADS // hp),
        in_specs=[pl.BlockSpec((seq, hp * HEAD_BLOCK), lambda b, h: (b, h)),
                  pl.BlockSpec((seq, hp * HEAD_BLOCK), lambda b, h: (b, h)),
                  pl.BlockSpec((seq, hp * V_DIM), lambda b, h: (b, h))],
        out_specs=pl.BlockSpec((seq, hp * V_DIM), lambda b, h: (b, h)),
        out_shape=jax.ShapeDtypeStruct((t, N_HEADS * V_DIM), BF16),
        compiler_params=_params(("parallel", "parallel")),
        name="attn",
    )(q, k, v)


def _merge_kernel(x_ref, g_ref, wg_ref, bg_ref, pool_ref, wpp_ref, attn_ref, wap_ref, wout_ref, o_ref):
    tm, d = x_ref.shape
    ts = tm // MERGE_SUB_TILES
    for r0 in range(0, tm, ts):
        rows = slice(r0, r0 + ts)
        x = x_ref[rows, :]
        h = _rms(x, g_ref[...]).astype(BF16)
        gates = jax.nn.sigmoid(_dot_t(h, wg_ref[...]) + bg_ref[...])
        branch_a = _dot(pool_ref[rows, :], wpp_ref[...])
        branch_b = _dot(attn_ref[rows, :], wap_ref[...])
        merged = gates[:, :d] * branch_a + gates[:, d:] * branch_b
        o_ref[rows, :] = x + _dot(merged.astype(BF16), wout_ref[...])


def _merge(x, norm, w_gate, b_gate, pool, wpp, attn, wap, wout, layer):
    t, d = x.shape
    tm = min(MERGE_TOKEN_TILE, t)
    lay3 = lambda i: (layer, 0, 0)
    row = lambda i: (i, 0)
    resident = pl.Buffered(1)
    return pl.pallas_call(
        _merge_kernel,
        grid=(t // tm,),
        in_specs=[
            pl.BlockSpec((tm, d), row),
            pl.BlockSpec((None, 1, d), lay3),
            pl.BlockSpec((None,) + w_gate.shape[1:], lay3, pipeline_mode=resident),
            pl.BlockSpec((None, 1, 2 * d), lay3),
            pl.BlockSpec((tm, pool.shape[1]), row),
            pl.BlockSpec((None,) + wpp.shape[1:], lay3, pipeline_mode=resident),
            pl.BlockSpec((tm, attn.shape[1]), row),
            pl.BlockSpec((None,) + wap.shape[1:], lay3, pipeline_mode=resident),
            pl.BlockSpec((None,) + wout.shape[1:], lay3, pipeline_mode=resident),
        ],
        out_specs=pl.BlockSpec((tm, d), row),
        out_shape=jax.ShapeDtypeStruct((t, d), F32),
        compiler_params=_params(("parallel",)),
        name="merge",
    )(x, norm, w_gate, b_gate, pool, wpp, attn, wap, wout)


def _prep_w_in(w_in):
    n_lat = POOL_DIM + 384 + 256
    w_t = jnp.swapaxes(w_in, 1, 2)
    w_lat = jnp.concatenate([w_t[:, :n_lat + ROPE], w_t[:, n_lat:n_lat + ROPE]], axis=1)
    return w_lat.astype(BF16), w_t[:, n_lat + ROPE:].astype(BF16)


def _prep_w_uq(w_uq):
    l, r, _ = w_uq.shape
    w = w_uq.reshape(l, r, N_HEADS, QK_DIM)
    nope = w[..., :NOPE].reshape(l, r, N_HEADS * NOPE)
    rope = w[..., NOPE:]
    return jnp.concatenate([nope, rope.reshape(l, r, N_HEADS * ROPE)], axis=-1).astype(BF16)


def _prep_w_ukv(w_ukv):
    l, r, _ = w_ukv.shape
    w = w_ukv.reshape(l, r, N_HEADS, 2, NOPE)
    return jnp.swapaxes(w, 2, 3).reshape(l, r, 2 * N_HEADS * NOPE).astype(BF16)


def kernel(x, positions, norm_ffn1, ffn1_up, ffn1_down, norm_mix, w_in, b_gate, pool_maps, pool_scale, w_pool_proj,
           q_latent_norm, w_uq, kv_latent_norm, w_ukv, w_attn_proj, w_out, norm_ffn2, ffn2_up, ffn2_down, final_norm):
    batch, seq, d = x.shape
    depth = norm_ffn1.shape[0]
    t = batch * seq
    assert seq % 128 == 0 and d == 1024

    row3 = lambda a: a.reshape(a.shape[0], 1, a.shape[1])
    w_lat, w_gate = _prep_w_in(w_in)
    wq = _prep_w_uq(w_uq)
    wkv = _prep_w_ukv(w_ukv)
    up1, down1 = ffn1_up.astype(BF16), ffn1_down.astype(BF16)
    up2, down2 = ffn2_up.astype(BF16), ffn2_down.astype(BF16)
    pmaps = pool_maps.astype(BF16)
    wpp, wap, wout = w_pool_proj.astype(BF16), w_attn_proj.astype(BF16), w_out.astype(BF16)
    n1, nm, n2 = row3(norm_ffn1), row3(norm_mix), row3(norm_ffn2)
    qn, kvn, ps, bg = row3(q_latent_norm), row3(kv_latent_norm), row3(pool_scale), row3(b_gate)
    gf = final_norm.reshape(1, d)

    cos, sin = _rope_tables(positions)
    xt = x.reshape(t, d)
    for layer in range(depth):
        xt = _ffn(xt, n1, up1, down1, layer)
        pool, q, k, v = _mix_in(xt, seq, nm, w_lat, qn, wq, kvn, wkv, pmaps, ps, cos, sin, layer)
        attn = _attention(q, k, v, batch, seq)
        xt = _merge(xt, nm, w_gate, bg, pool, wpp, attn, wap, wout, layer)
        xt = _ffn(xt, n2, up2, down2, layer, final_norm=gf if layer == depth - 1 else None)
    return xt.reshape(batch, seq, d)
```

```python
import functools
import math

import jax
import jax.numpy as jnp
from jax import lax
from jax.experimental import pallas as pl
from jax.experimental.pallas import tpu as pltpu

F32 = jnp.float32
BF16 = jnp.bfloat16

N_HEADS = 8
NOPE = 128
ROPE = 64
HALF = ROPE // 2
V_DIM = 128
QK_DIM = NOPE + ROPE
HEAD_BLOCK = 256
ROPE_THETA = 10000.0
POOL_WINDOWS = (2, 4, 8, 16)
POOL_GROUP = 128
POOL_DIM = len(POOL_WINDOWS) * POOL_GROUP
POOL_CARRY = 32
NORM_EPS = 1e-6
LANES = 128
BF16_SUBLANES = 16
NEG_BIG = -1e30

ROPE_TABLE_BLOCK = 4096
FFN_TOKEN_TILE = 1024
FFN_FF_TILE = 256
MIX_TOKEN_TILE = 512
MIX_SUB_TILES = 2
MERGE_TOKEN_TILE = 1024
MERGE_SUB_TILES = 2
ATTN_TILE = 512
ATTN_KEY_TILE = 512
ATTN_HEADS_PER_STEP = 2
VMEM_LIMIT = 56 * 1024 * 1024


def _rms(x, g):
    ms = jnp.mean(x * x, axis=-1, keepdims=True)
    return x * lax.rsqrt(ms + NORM_EPS) * g


def _dot(a, b):
    return jnp.dot(a, b, preferred_element_type=F32)


def _dot_t(a, b):
    return lax.dot_general(a, b, (((1,), (1,)), ((), ())), preferred_element_type=F32)


def _rotate_half(x, first_half):
    return jnp.where(first_half, -pltpu.roll(x, LANES - HALF, 1), pltpu.roll(x, HALF, 1))


def _params(sem):
    return pltpu.CompilerParams(dimension_semantics=sem, vmem_limit_bytes=VMEM_LIMIT)


def _rope_table_kernel(pos_ref, invf_ref, cos_ref, sin_ref):
    rows = pos_ref.shape[0]
    groups = LANES // HALF
    ang = pos_ref[...].astype(F32) * invf_ref[...]
    group = lax.broadcasted_iota(jnp.int32, (rows, LANES), 1) // HALF
    for table, out_ref in ((jnp.cos(ang), cos_ref), (jnp.sin(ang), sin_ref)):
        rolled = [table] + [pltpu.roll(table, HALF * k, 1) for k in range(1, groups)]
        for g in range(groups):
            spread = rolled[(-g) % groups]
            for k in range(1, groups):
                spread = jnp.where(group == k, rolled[(k - g) % groups], spread)
            out_ref[g * rows:(g + 1) * rows, :] = spread


def _rope_tables(positions):
    t = positions.size
    inv_freq = ROPE_THETA ** (-jnp.arange(0, ROPE, 2, dtype=F32) / ROPE)
    groups = LANES // HALF
    block = min(t, ROPE_TABLE_BLOCK)
    rows = block // groups
    pos = positions.reshape(t // block, groups, rows)
    pos = jnp.repeat(jnp.swapaxes(pos, 1, 2), HALF, axis=2).reshape(t // groups, LANES)
    invf = jnp.tile(inv_freq, groups).reshape(1, LANES)
    return pl.pallas_call(
        _rope_table_kernel,
        grid=(t // block,),
        in_specs=[pl.BlockSpec((rows, LANES), lambda i: (i, 0)),
                  pl.BlockSpec((1, LANES), lambda i: (0, 0))],
        out_specs=[pl.BlockSpec((block, LANES), lambda i: (i, 0))] * 2,
        out_shape=[jax.ShapeDtypeStruct((t, LANES), F32)] * 2,
        compiler_params=_params(("parallel",)),
        name="rope_tables",
    )(pos, invf)


def _ffn_kernel(*refs, d_ff, tf, final):
    if final:
        x_ref, g_ref, wup_ref, wd_ref, gf_ref, o_ref = refs
    else:
        x_ref, g_ref, wup_ref, wd_ref, next_up_ref, next_down_ref, o_ref, next_up_o, next_down_o = refs
        next_up_o[...] = next_up_ref[...].astype(BF16)
        next_down_o[...] = next_down_ref[...].astype(BF16)
    x = x_ref[...]
    h = _rms(x, g_ref[...]).astype(BF16)
    acc = None
    for lo in range(0, d_ff, tf):
        gate = _dot(h, wup_ref[:, lo:lo + tf])
        up = _dot(h, wup_ref[:, d_ff + lo:d_ff + lo + tf])
        act = (gate * jax.nn.sigmoid(gate) * up).astype(BF16)
        part = _dot(act, wd_ref[lo:lo + tf, :])
        acc = part if acc is None else acc + part
    y = x + 0.5 * acc
    if final:
        y = _rms(y, gf_ref[...])
    o_ref[...] = y


def _ffn(x, norm, layer, w_up, w_down, next_weights=None, final_norm=None):
    t, d = x.shape
    f = w_down.shape[0]
    tm = min(FFN_TOKEN_TILE, t)
    steps = t // tm
    final = final_norm is not None
    resident = pl.Buffered(1)
    in_specs = [
        pl.BlockSpec((tm, d), lambda i: (i, 0)),
        pl.BlockSpec((None, 1, d), lambda i: (layer, 0, 0)),
        pl.BlockSpec((d, 2 * f), lambda i: (0, 0), pipeline_mode=resident),
        pl.BlockSpec((f, d), lambda i: (0, 0), pipeline_mode=resident),
    ]
    args = [x, norm, w_up, w_down]
    out_specs = [pl.BlockSpec((tm, d), lambda i: (i, 0))]
    out_shape = [jax.ShapeDtypeStruct((t, d), F32)]
    if final:
        in_specs.append(pl.BlockSpec((1, d), lambda i: (0, 0)))
        args.append(final_norm)
    else:
        next_up, next_down, nl = next_weights
        up_slabs = math.gcd(steps, d // BF16_SUBLANES)
        down_slabs = math.gcd(steps, f // BF16_SUBLANES)
        up_rows, down_rows = d // up_slabs, f // down_slabs
        up_idx = lambda i: i // (steps // up_slabs)
        down_idx = lambda i: i // (steps // down_slabs)
        in_specs += [pl.BlockSpec((None, up_rows, 2 * f), lambda i: (nl, up_idx(i), 0)),
                     pl.BlockSpec((None, down_rows, d), lambda i: (nl, down_idx(i), 0))]
        args += [next_up, next_down]
        out_specs += [pl.BlockSpec((up_rows, 2 * f), lambda i: (up_idx(i), 0)),
                      pl.BlockSpec((down_rows, d), lambda i: (down_idx(i), 0))]
        out_shape += [jax.ShapeDtypeStruct((d, 2 * f), BF16), jax.ShapeDtypeStruct((f, d), BF16)]
    out = pl.pallas_call(
        functools.partial(_ffn_kernel, d_ff=f, tf=FFN_FF_TILE, final=final),
        grid=(steps,),
        in_specs=in_specs,
        out_specs=out_specs,
        out_shape=out_shape,
        compiler_params=_params(("arbitrary",)),
        name="ffn_final" if final else "ffn",
    )(*args)
    return (out[0], None) if final else (out[0], (out[1], out[2]))


def _mix_in_kernel(x_ref, g_ref, wl_ref, wkr_ref, qn_ref, wq_ref, kvn_ref, wkv_ref, pm_ref, ps_ref, cos_ref, sin_ref,
                   pool_ref, q_ref, k_ref, v_ref, e_ref, t1_ref, t2_ref, t3_ref, *, tm, tiles_per_seq):
    c = POOL_CARRY
    g = POOL_GROUP
    seq_tile = lax.rem(pl.program_id(0), tiles_per_seq)

    @pl.when(seq_tile == 0)
    def _():
        e_ref[0:c, :] = jnp.zeros((c, POOL_DIM), F32)

    ts = tm // MIX_SUB_TILES
    lane = lax.broadcasted_iota(jnp.int32, (ts, LANES), 1)
    first_half = (lane & (ROPE - 1)) < HALF
    n_all = N_HEADS * NOPE
    for r0 in range(0, tm, ts):
        rows = slice(r0, r0 + ts)
        h = _rms(x_ref[rows, :], g_ref[...]).astype(BF16)
        p = _dot_t(h, wl_ref[...])
        kr_main = _dot_t(h, wkr_ref[...])
        e_ref[c + r0:c + r0 + ts, :] = p[:, :POOL_DIM]
        q_lat = p[:, POOL_DIM:POOL_DIM + 384]
        kv_lat = p[:, POOL_DIM + 384:POOL_DIM + 640]
        cos = cos_ref[rows, :]
        sin = sin_ref[rows, :]

        qa = _dot(_rms(q_lat, qn_ref[...]).astype(BF16), wq_ref[...])
        for pair in range(N_HEADS // 2):
            main = qa[:, n_all + pair * LANES:n_all + (pair + 1) * LANES]
            roped = main * cos + _rotate_half(main, first_half) * sin
            for sub in range(2):
                head = 2 * pair + sub
                keep = (lane < ROPE) if sub == 0 else (lane >= ROPE)
                base = head * HEAD_BLOCK
                q_ref[rows, base:base + NOPE] = qa[:, head * NOPE:(head + 1) * NOPE].astype(BF16)
                q_ref[rows, base + NOPE:base + HEAD_BLOCK] = jnp.where(keep, roped, 0.0).astype(BF16)

        kr = (kr_main * cos + _rotate_half(kr_main, first_half) * sin).astype(BF16)
        kv = _dot(_rms(kv_lat, kvn_ref[...]).astype(BF16), wkv_ref[...])
        for head in range(N_HEADS):
            base = head * HEAD_BLOCK
            k_ref[rows, base:base + NOPE] = kv[:, head * NOPE:(head + 1) * NOPE].astype(BF16)
            k_ref[rows, base + NOPE:base + HEAD_BLOCK] = kr
        v_ref[rows, :] = kv[:, n_all:].astype(BF16)

    xp = e_ref[c:c + tm, :]
    t1_ref[8:c + tm, :] = e_ref[8:c + tm, :] + e_ref[7:c + tm - 1, :]
    t2_ref[16:c + tm, :] = t1_ref[16:c + tm, g:4 * g] + t1_ref[14:c + tm - 2, g:4 * g]
    t3_ref[24:c + tm, :] = t2_ref[24:c + tm, g:3 * g] + t2_ref[20:c + tm - 4, g:3 * g]
    w16 = t3_ref[c:c + tm, g:2 * g] + t3_ref[c - 8:c + tm - 8, g:2 * g]
    sums = (t1_ref[c:c + tm, 0:g], t2_ref[c:c + tm, 0:g], t3_ref[c:c + tm, 0:g], w16)
    e_ref[0:c, :] = e_ref[tm:tm + c, :]

    pos1 = (seq_tile * tm + 1 + lax.broadcasted_iota(jnp.int32, (tm, 1), 0)).astype(F32)
    for gi, w in enumerate(POOL_WINDOWS):
        count = jnp.minimum(pos1, float(w))
        pooled = sums[gi] / count - xp[:, gi * g:(gi + 1) * g]
        mixed = _dot(pooled.astype(BF16), pm_ref[gi]) * ps_ref[:, gi * g:(gi + 1) * g]
        pool_ref[:, gi * g:(gi + 1) * g] = mixed.astype(BF16)


def _mix_in(x, seq, norm, w_lat, w_kr, qn, wq, kvn, wkv, pmaps, pscale, cos, sin, layer):
    t, d = x.shape
    tm = min(MIX_TOKEN_TILE, seq)
    tiles_per_seq = seq // tm
    c = POOL_CARRY
    lay3 = lambda i: (layer, 0, 0)
    row = lambda i: (i, 0)
    in_specs = [
        pl.BlockSpec((tm, d), row),
        pl.BlockSpec((None, 1, d), lay3),
        pl.BlockSpec((None,) + w_lat.shape[1:], lay3),
        pl.BlockSpec((None,) + w_kr.shape[1:], lay3),
        pl.BlockSpec((None, 1, qn.shape[2]), lay3),
        pl.BlockSpec((None,) + wq.shape[1:], lay3),
        pl.BlockSpec((None, 1, kvn.shape[2]), lay3),
        pl.BlockSpec((None,) + wkv.shape[1:], lay3),
        pl.BlockSpec((None,) + pmaps.shape[1:], lambda i: (layer, 0, 0, 0)),
        pl.BlockSpec((None, 1, POOL_DIM), lay3),
        pl.BlockSpec((tm, LANES), row),
        pl.BlockSpec((tm, LANES), row),
    ]
    qk_w = N_HEADS * HEAD_BLOCK
    out_shape = [jax.ShapeDtypeStruct((t, POOL_DIM), BF16), jax.ShapeDtypeStruct((t, qk_w), BF16),
                 jax.ShapeDtypeStruct((t, qk_w), BF16), jax.ShapeDtypeStruct((t, N_HEADS * V_DIM), BF16)]
    out_specs = [pl.BlockSpec((tm, POOL_DIM), row), pl.BlockSpec((tm, qk_w), row),
                 pl.BlockSpec((tm, qk_w), row), pl.BlockSpec((tm, N_HEADS * V_DIM), row)]
    g = POOL_GROUP
    return pl.pallas_call(
        functools.partial(_mix_in_kernel, tm=tm, tiles_per_seq=tiles_per_seq),
        grid=(t // tm,),
        in_specs=in_specs,
        out_specs=out_specs,
        out_shape=out_shape,
        scratch_shapes=[pltpu.VMEM((c + tm, 4 * g), F32), pltpu.VMEM((c + tm, 4 * g), F32),
                        pltpu.VMEM((c + tm, 3 * g), F32), pltpu.VMEM((c + tm, 2 * g), F32)],
        compiler_params=_params(("arbitrary",)),
        name="mix_in",
    )(x, norm, w_lat, w_kr, qn, wq, kvn, wkv, pmaps, pscale, cos, sin)


def _attn_kernel(q_ref, k_ref, v_ref, o_ref, *, seq, tile, heads, exp2_scale):
    tq, tk = tile, ATTN_KEY_TILE
    row = lax.broadcasted_iota(jnp.int32, (tq, tk), 0)
    col = lax.broadcasted_iota(jnp.int32, (tq, tk), 1)
    for qi in range(seq // tq):
        q0 = qi * tq
        for hh in range(heads):
            qk_cols = slice(hh * HEAD_BLOCK, (hh + 1) * HEAD_BLOCK)
            v_cols = slice(hh * V_DIM, (hh + 1) * V_DIM)
            q = q_ref[q0:q0 + tq, qk_cols]
            m = l = acc = None
            for k0 in range(0, q0 + tq, tk):
                k = k_ref[k0:k0 + tk, qk_cols]
                v = v_ref[k0:k0 + tk, v_cols]
                s = _dot_t(q, k)
                if k0 + tk - 1 > q0:
                    s = jnp.where(row + q0 >= col + k0, s, NEG_BIG)
                s_max = jnp.max(s, axis=-1, keepdims=True)
                if k0 == 0:
                    m = s_max
                    p = jnp.exp2((s - m) * exp2_scale)
                    l = jnp.sum(p, axis=-1, keepdims=True)
                    acc = _dot(p.astype(BF16), v)
                else:
                    m_new = jnp.maximum(m, s_max)
                    alpha = jnp.exp2((m - m_new) * exp2_scale)
                    p = jnp.exp2((s - m_new) * exp2_scale)
                    l = alpha * l + jnp.sum(p, axis=-1, keepdims=True)
                    acc = alpha * acc + _dot(p.astype(BF16), v)
                    m = m_new
            o_ref[q0:q0 + tq, v_cols] = (acc / l).astype(BF16)


def _attention(q, k, v, batch, seq):
    t = q.shape[0]
    tile = min(ATTN_TILE, seq)
    exp2_scale = (QK_DIM ** -0.5) * math.log2(math.e)
    hp = ATTN_HEADS_PER_STEP
    return pl.pallas_call(
        functools.partial(_attn_kernel, seq=seq, tile=tile, heads=hp, exp2_scale=exp2_scale),
        grid=(batch, N_HEADS // hp),
        in_specs=[pl.BlockSpec((seq, hp * HEAD_BLOCK), lambda b, h: (b, h)),
                  pl.BlockSpec((seq, hp * HEAD_BLOCK), lambda b, h: (b, h)),
                  pl.BlockSpec((seq, hp * V_DIM), lambda b, h: (b, h))],
        out_specs=pl.BlockSpec((seq, hp * V_DIM), lambda b, h: (b, h)),
        out_shape=jax.ShapeDtypeStruct((t, N_HEADS * V_DIM), BF16),
        compiler_params=_params(("parallel", "parallel")),
        name="attn",
    )(q, k, v)


def _merge_kernel(x_ref, g_ref, wg_ref, bg_ref, pool_ref, wpp_ref, attn_ref, wap_ref, wout_ref, o_ref):
    tm, d = x_ref.shape
    ts = tm // MERGE_SUB_TILES
    for r0 in range(0, tm, ts):
        rows = slice(r0, r0 + ts)
        x = x_ref[rows, :]
        h = _rms(x, g_ref[...]).astype(BF16)
        gates = jax.nn.sigmoid(_dot_t(h, wg_ref[...]) + bg_ref[...])
        branch_a = _dot(pool_ref[rows, :], wpp_ref[...])
        branch_b = _dot(attn_ref[rows, :], wap_ref[...])
        merged = gates[:, :d] * branch_a + gates[:, d:] * branch_b
        o_ref[rows, :] = x + _dot(merged.astype(BF16), wout_ref[...])


def _merge(x, norm, w_gate, b_gate, pool, wpp, attn, wap, wout, layer):
    t, d = x.shape
    tm = min(MERGE_TOKEN_TILE, t)
    lay3 = lambda i: (layer, 0, 0)
    row = lambda i: (i, 0)
    resident = pl.Buffered(1)
    return pl.pallas_call(
        _merge_kernel,
        grid=(t // tm,),
        in_specs=[
            pl.BlockSpec((tm, d), row),
            pl.BlockSpec((None, 1, d), lay3),
            pl.BlockSpec((None,) + w_gate.shape[1:], lay3, pipeline_mode=resident),
            pl.BlockSpec((None, 1, 2 * d), lay3),
            pl.BlockSpec((tm, pool.shape[1]), row),
            pl.BlockSpec((None,) + wpp.shape[1:], lay3, pipeline_mode=resident),
            pl.BlockSpec((tm, attn.shape[1]), row),
            pl.BlockSpec((None,) + wap.shape[1:], lay3, pipeline_mode=resident),
            pl.BlockSpec((None,) + wout.shape[1:], lay3, pipeline_mode=resident),
        ],
        out_specs=pl.BlockSpec((tm, d), row),
        out_shape=jax.ShapeDtypeStruct((t, d), F32),
        compiler_params=_params(("parallel",)),
        name="merge",
    )(x, norm, w_gate, b_gate, pool, wpp, attn, wap, wout)


def _prep_w_in(w_in):
    n_lat = POOL_DIM + 384 + 256
    w_t = jnp.swapaxes(w_in, 1, 2)
    kr = w_t[:, n_lat:n_lat + ROPE]
    w_kr = jnp.concatenate([kr, kr], axis=1)
    return w_t[:, :n_lat].astype(BF16), w_kr.astype(BF16), w_t[:, n_lat + ROPE:].astype(BF16)


def _prep_w_uq(w_uq):
    l, r, _ = w_uq.shape
    w = w_uq.reshape(l, r, N_HEADS, QK_DIM)
    nope = w[..., :NOPE].reshape(l, r, N_HEADS * NOPE)
    rope = w[..., NOPE:]
    return jnp.concatenate([nope, rope.reshape(l, r, N_HEADS * ROPE)], axis=-1).astype(BF16)


def _prep_w_ukv(w_ukv):
    l, r, _ = w_ukv.shape
    w = w_ukv.reshape(l, r, N_HEADS, 2, NOPE)
    return jnp.swapaxes(w, 2, 3).reshape(l, r, 2 * N_HEADS * NOPE).astype(BF16)


def kernel(x, positions, norm_ffn1, ffn1_up, ffn1_down, norm_mix, w_in, b_gate, pool_maps, pool_scale, w_pool_proj,
           q_latent_norm, w_uq, kv_latent_norm, w_ukv, w_attn_proj, w_out, norm_ffn2, ffn2_up, ffn2_down, final_norm):
    batch, seq, d = x.shape
    depth = norm_ffn1.shape[0]
    t = batch * seq
    assert seq % 128 == 0 and d == 1024

    row3 = lambda a: a.reshape(a.shape[0], 1, a.shape[1])
    w_lat, w_kr, w_gate = _prep_w_in(w_in)
    wq = _prep_w_uq(w_uq)
    wkv = _prep_w_ukv(w_ukv)
    pmaps = pool_maps.astype(BF16)
    wpp, wap, wout = w_pool_proj.astype(BF16), w_attn_proj.astype(BF16), w_out.astype(BF16)
    n1, nm, n2 = row3(norm_ffn1), row3(norm_mix), row3(norm_ffn2)
    qn, kvn, ps, bg = row3(q_latent_norm), row3(kv_latent_norm), row3(pool_scale), row3(b_gate)
    gf = final_norm.reshape(1, d)

    cos, sin = _rope_tables(positions)
    xt = x.reshape(t, d)
    w_ffn = (ffn1_up[0].astype(BF16), ffn1_down[0].astype(BF16))
    for layer in range(depth):
        xt, w_ffn = _ffn(xt, n1, layer, *w_ffn, next_weights=(ffn2_up, ffn2_down, layer))
        pool, q, k, v = _mix_in(xt, seq, nm, w_lat, w_kr, qn, wq, kvn, wkv, pmaps, ps, cos, sin, layer)
        attn = _attention(q, k, v, batch, seq)
        xt = _merge(xt, nm, w_gate, bg, pool, wpp, attn, wap, wout, layer)
        if layer == depth - 1:
            xt, _ = _ffn(xt, n2, layer, *w_ffn, final_norm=gf)
        else:
            xt, w_ffn = _ffn(xt, n2, layer, *w_ffn, next_weights=(ffn1_up, ffn1_down, layer + 1))
    return xt.reshape(batch, seq, d)
```

```python
import functools
import math

import jax
import jax.numpy as jnp
from jax import lax
from jax.experimental import pallas as pl
from jax.experimental.pallas import tpu as pltpu

F32 = jnp.float32
BF16 = jnp.bfloat16

N_HEADS = 8
NOPE = 128
ROPE = 64
HALF = ROPE // 2
V_DIM = 128
QK_DIM = NOPE + ROPE
ROPE_THETA = 10000.0
POOL_WINDOWS = (2, 4, 8, 16)
POOL_GROUP = 128
POOL_DIM = len(POOL_WINDOWS) * POOL_GROUP
POOL_CARRY = 32
NORM_EPS = 1e-6
LANES = 128
BF16_SUBLANES = 16
NEG_BIG = -1e30

ROPE_TABLE_BLOCK = 4096
FFN_TOKEN_TILE = 1024
FFN_FF_TILE = 256
MIX_TOKEN_TILE = 512
MIX_SUB_TILES = 2
MERGE_TOKEN_TILE = 1024
MERGE_SUB_TILES = 2
ATTN_TILE = 512
ATTN_KEY_TILE = 512
VMEM_LIMIT = 56 * 1024 * 1024


def _rms(x, g):
    ms = jnp.mean(x * x, axis=-1, keepdims=True)
    return x * lax.rsqrt(ms + NORM_EPS) * g


def _dot(a, b):
    return jnp.dot(a, b, preferred_element_type=F32)


def _dot_t(a, b):
    return lax.dot_general(a, b, (((1,), (1,)), ((), ())), preferred_element_type=F32)


def _rotate_half(x, first_half):
    return jnp.where(first_half, -pltpu.roll(x, LANES - HALF, 1), pltpu.roll(x, HALF, 1))


def _params(sem):
    return pltpu.CompilerParams(dimension_semantics=sem, vmem_limit_bytes=VMEM_LIMIT)


def _rope_table_kernel(pos_ref, invf_ref, cos_ref, sin_ref):
    rows = pos_ref.shape[0]
    groups = LANES // HALF
    ang = pos_ref[...].astype(F32) * invf_ref[...]
    group = lax.broadcasted_iota(jnp.int32, (rows, LANES), 1) // HALF
    for table, out_ref in ((jnp.cos(ang), cos_ref), (jnp.sin(ang), sin_ref)):
        rolled = [table] + [pltpu.roll(table, HALF * k, 1) for k in range(1, groups)]
        for g in range(groups):
            spread = rolled[(-g) % groups]
            for k in range(1, groups):
                spread = jnp.where(group == k, rolled[(k - g) % groups], spread)
            out_ref[g * rows:(g + 1) * rows, :] = spread


def _rope_tables(positions):
    t = positions.size
    inv_freq = ROPE_THETA ** (-jnp.arange(0, ROPE, 2, dtype=F32) / ROPE)
    groups = LANES // HALF
    block = min(t, ROPE_TABLE_BLOCK)
    rows = block // groups
    pos = positions.reshape(t // block, groups, rows)
    pos = jnp.repeat(jnp.swapaxes(pos, 1, 2), HALF, axis=2).reshape(t // groups, LANES)
    invf = jnp.tile(inv_freq, groups).reshape(1, LANES)
    return pl.pallas_call(
        _rope_table_kernel,
        grid=(t // block,),
        in_specs=[pl.BlockSpec((rows, LANES), lambda i: (i, 0)),
                  pl.BlockSpec((1, LANES), lambda i: (0, 0))],
        out_specs=[pl.BlockSpec((block, LANES), lambda i: (i, 0))] * 2,
        out_shape=[jax.ShapeDtypeStruct((t, LANES), F32)] * 2,
        compiler_params=_params(("parallel",)),
        name="rope_tables",
    )(pos, invf)


def _ffn_kernel(*refs, d_ff, tf, final, n_casts):
    x_ref, g_ref, wup_ref, wd_ref = refs[:4]
    if final:
        gf_ref, o_ref = refs[4:]
    else:
        o_ref = refs[4 + n_casts]
        for src, dst in zip(refs[4:4 + n_casts], refs[5 + n_casts:]):
            dst[...] = src[...].astype(BF16)
    x = x_ref[...]
    h = _rms(x, g_ref[...]).astype(BF16)
    acc = None
    for lo in range(0, d_ff, tf):
        gate = _dot(h, wup_ref[:, lo:lo + tf])
        up = _dot(h, wup_ref[:, d_ff + lo:d_ff + lo + tf])
        act = (gate * jax.nn.sigmoid(gate) * up).astype(BF16)
        part = _dot(act, wd_ref[lo:lo + tf, :])
        acc = part if acc is None else acc + part
    y = x + 0.5 * acc
    if final:
        y = _rms(y, gf_ref[...])
    o_ref[...] = y


def _ffn(x, norm, layer, w_up, w_down, casts=(), final_norm=None):
    t, d = x.shape
    f = w_down.shape[0]
    tm = min(FFN_TOKEN_TILE, t)
    steps = t // tm
    final = final_norm is not None
    resident = pl.Buffered(1)
    in_specs = [
        pl.BlockSpec((tm, d), lambda i: (i, 0)),
        pl.BlockSpec((None, 1, d), lambda i: (layer, 0, 0)),
        pl.BlockSpec((d, 2 * f), lambda i: (0, 0), pipeline_mode=resident),
        pl.BlockSpec((f, d), lambda i: (0, 0), pipeline_mode=resident),
    ]
    args = [x, norm, w_up, w_down]
    out_specs = [pl.BlockSpec((tm, d), lambda i: (i, 0))]
    out_shape = [jax.ShapeDtypeStruct((t, d), F32)]
    if final:
        assert not casts
        in_specs.append(pl.BlockSpec((1, d), lambda i: (0, 0)))
        args.append(final_norm)
    in_kernel = []
    for job, (src, src_layer, first_row, n_rows) in enumerate(casts):
        slabs = math.gcd(steps, n_rows // BF16_SUBLANES)
        slab_rows = n_rows // slabs
        if n_rows % (slabs * BF16_SUBLANES) or first_row % slab_rows:
            continue
        in_kernel.append(job)
        width = src.shape[2]
        in_specs.append(pl.BlockSpec(
            (None, slab_rows, width),
            lambda i, sl=src_layer, fs=first_row // slab_rows, rep=steps // slabs: (sl, fs + i // rep, 0)))
        args.append(src)
        out_specs.append(pl.BlockSpec((slab_rows, width), lambda i, rep=steps // slabs: (i // rep, 0)))
        out_shape.append(jax.ShapeDtypeStruct((n_rows, width), BF16))
    out = pl.pallas_call(
        functools.partial(_ffn_kernel, d_ff=f, tf=FFN_FF_TILE, final=final, n_casts=len(in_kernel)),
        grid=(steps,),
        in_specs=in_specs,
        out_specs=out_specs,
        out_shape=out_shape,
        compiler_params=_params(("arbitrary",)),
        name="ffn_final" if final else "ffn",
    )(*args)
    done = dict(zip(in_kernel, out[1:]))
    cast = [done[job] if job in done else src[sl, first:first + n].astype(BF16)
            for job, (src, sl, first, n) in enumerate(casts)]
    return out[0], cast


def _mix_in_kernel(x_ref, g_ref, wl_ref, wkr_ref, qn_ref, wq_ref, kvn_ref, wkv_ref, pm_ref, ps_ref, cos_ref, sin_ref,
                   pool_ref, qnope_ref, qrope_ref, knope_ref, krope_ref, v_ref, e_ref, t1_ref, t2_ref, t3_ref,
                   *, tm, tiles_per_seq):
    c = POOL_CARRY
    g = POOL_GROUP
    seq_tile = lax.rem(pl.program_id(0), tiles_per_seq)

    @pl.when(seq_tile == 0)
    def _():
        e_ref[0:c, :] = jnp.zeros((c, POOL_DIM), F32)

    ts = tm // MIX_SUB_TILES
    lane = lax.broadcasted_iota(jnp.int32, (ts, LANES), 1)
    first_half = (lane & (ROPE - 1)) < HALF
    n_all = N_HEADS * NOPE
    for r0 in range(0, tm, ts):
        rows = slice(r0, r0 + ts)
        h = _rms(x_ref[rows, :], g_ref[...]).astype(BF16)
        p = _dot_t(h, wl_ref[...])
        kr_main = _dot_t(h, wkr_ref[...])
        e_ref[c + r0:c + r0 + ts, :] = p[:, :POOL_DIM]
        q_lat = p[:, POOL_DIM:POOL_DIM + 384]
        kv_lat = p[:, POOL_DIM + 384:POOL_DIM + 640]
        cos = cos_ref[rows, :]
        sin = sin_ref[rows, :]

        qa = _dot(_rms(q_lat, qn_ref[...]).astype(BF16), wq_ref[...])
        qnope_ref[rows, :] = qa[:, :n_all].astype(BF16)
        for pair in range(N_HEADS // 2):
            main = qa[:, n_all + pair * LANES:n_all + (pair + 1) * LANES]
            roped = main * cos + _rotate_half(main, first_half) * sin
            qrope_ref[rows, pair * LANES:(pair + 1) * LANES] = roped.astype(BF16)

        krope_ref[rows, :] = (kr_main * cos + _rotate_half(kr_main, first_half) * sin).astype(BF16)
        kv = _dot(_rms(kv_lat, kvn_ref[...]).astype(BF16), wkv_ref[...])
        knope_ref[rows, :] = kv[:, :n_all].astype(BF16)
        v_ref[rows, :] = kv[:, n_all:].astype(BF16)

    xp = e_ref[c:c + tm, :]
    t1_ref[8:c + tm, :] = e_ref[8:c + tm, :] + e_ref[7:c + tm - 1, :]
    t2_ref[16:c + tm, :] = t1_ref[16:c + tm, g:4 * g] + t1_ref[14:c + tm - 2, g:4 * g]
    t3_ref[24:c + tm, :] = t2_ref[24:c + tm, g:3 * g] + t2_ref[20:c + tm - 4, g:3 * g]
    w16 = t3_ref[c:c + tm, g:2 * g] + t3_ref[c - 8:c + tm - 8, g:2 * g]
    sums = (t1_ref[c:c + tm, 0:g], t2_ref[c:c + tm, 0:g], t3_ref[c:c + tm, 0:g], w16)
    e_ref[0:c, :] = e_ref[tm:tm + c, :]

    pos1 = (seq_tile * tm + 1 + lax.broadcasted_iota(jnp.int32, (tm, 1), 0)).astype(F32)
    for gi, w in enumerate(POOL_WINDOWS):
        count = jnp.minimum(pos1, float(w))
        pooled = sums[gi] / count - xp[:, gi * g:(gi + 1) * g]
        mixed = _dot(pooled.astype(BF16), pm_ref[gi]) * ps_ref[:, gi * g:(gi + 1) * g]
        pool_ref[:, gi * g:(gi + 1) * g] = mixed.astype(BF16)


def _mix_in(x, seq, norm, w_lat, w_kr, qn, wq, kvn, wkv, pmaps, pscale, cos, sin, layer):
    t, d = x.shape
    tm = min(MIX_TOKEN_TILE, seq)
    tiles_per_seq = seq // tm
    c = POOL_CARRY
    lay3 = lambda i: (layer, 0, 0)
    row = lambda i: (i, 0)
    in_specs = [
        pl.BlockSpec((tm, d), row),
        pl.BlockSpec((None, 1, d), lay3),
        pl.BlockSpec(w_lat.shape, lambda i: (0, 0)),
        pl.BlockSpec((None,) + w_kr.shape[1:], lay3),
        pl.BlockSpec((None, 1, qn.shape[2]), lay3),
        pl.BlockSpec((None,) + wq.shape[1:], lay3),
        pl.BlockSpec((None, 1, kvn.shape[2]), lay3),
        pl.BlockSpec((None,) + wkv.shape[1:], lay3),
        pl.BlockSpec((None,) + pmaps.shape[1:], lambda i: (layer, 0, 0, 0)),
        pl.BlockSpec((None, 1, POOL_DIM), lay3),
        pl.BlockSpec((tm, LANES), row),
        pl.BlockSpec((tm, LANES), row),
    ]
    widths = (POOL_DIM, N_HEADS * NOPE, N_HEADS * ROPE, N_HEADS * NOPE, LANES, N_HEADS * V_DIM)
    out_shape = [jax.ShapeDtypeStruct((t, w), BF16) for w in widths]
    out_specs = [pl.BlockSpec((tm, w), row) for w in widths]
    g = POOL_GROUP
    return pl.pallas_call(
        functools.partial(_mix_in_kernel, tm=tm, tiles_per_seq=tiles_per_seq),
        grid=(t // tm,),
        in_specs=in_specs,
        out_specs=out_specs,
        out_shape=out_shape,
        scratch_shapes=[pltpu.VMEM((c + tm, 4 * g), F32), pltpu.VMEM((c + tm, 4 * g), F32),
                        pltpu.VMEM((c + tm, 3 * g), F32), pltpu.VMEM((c + tm, 2 * g), F32)],
        compiler_params=_params(("arbitrary",)),
        name="mix_in",
    )(x, norm, w_lat, w_kr, qn, wq, kvn, wkv, pmaps, pscale, cos, sin)


def _attn_kernel(qnope_ref, qrope_ref, knope_ref, krope_ref, v_ref, o_ref, *, seq, tile, exp2_scale):
    tq, tk = tile, ATTN_KEY_TILE
    row = lax.broadcasted_iota(jnp.int32, (tq, tk), 0)
    col = lax.broadcasted_iota(jnp.int32, (tq, tk), 1)
    lane = lax.broadcasted_iota(jnp.int32, (tq, LANES), 1)
    for qi in range(seq // tq):
        q0 = qi * tq
        q_pair = qrope_ref[q0:q0 + tq, :]
        for hh in range(2):
            cols = slice(hh * NOPE, (hh + 1) * NOPE)
            own = (lane < ROPE) if hh == 0 else (lane >= ROPE)
            q = jnp.concatenate([qnope_ref[q0:q0 + tq, cols], jnp.where(own, q_pair, jnp.zeros_like(q_pair))], axis=1)
            m = l = acc = None
            for k0 in range(0, q0 + tq, tk):
                k = jnp.concatenate([knope_ref[k0:k0 + tk, cols], krope_ref[k0:k0 + tk, :]], axis=1)
                v = v_ref[k0:k0 + tk, cols]
                s = _dot_t(q, k)
                if k0 + tk - 1 > q0:
                    s = jnp.where(row + q0 >= col + k0, s, NEG_BIG)
                s_max = jnp.max(s, axis=-1, keepdims=True)
                if k0 == 0:
                    m = s_max
                    p = jnp.exp2((s - m) * exp2_scale)
                    l = jnp.sum(p, axis=-1, keepdims=True)
                    acc = _dot(p.astype(BF16), v)
                else:
                    m_new = jnp.maximum(m, s_max)
                    alpha = jnp.exp2((m - m_new) * exp2_scale)
                    p = jnp.exp2((s - m_new) * exp2_scale)
                    l = alpha * l + jnp.sum(p, axis=-1, keepdims=True)
                    acc = alpha * acc + _dot(p.astype(BF16), v)
                    m = m_new
            o_ref[q0:q0 + tq, cols] = (acc / l).astype(BF16)


def _attention(q_nope, q_rope, k_nope, k_rope, v, batch, seq):
    t = v.shape[0]
    tile = min(ATTN_TILE, seq)
    exp2_scale = (QK_DIM ** -0.5) * math.log2(math.e)
    pair = lambda b, h: (b, h)
    return pl.pallas_call(
        functools.partial(_attn_kernel, seq=seq, tile=tile, exp2_scale=exp2_scale),
        grid=(batch, N_HEADS // 2),
        in_specs=[pl.BlockSpec((seq, 2 * NOPE), pair),
                  pl.BlockSpec((seq, LANES), pair),
                  pl.BlockSpec((seq, 2 * NOPE), pair),
                  pl.BlockSpec((seq, LANES), lambda b, h: (b, 0)),
                  pl.BlockSpec((seq, 2 * V_DIM), pair)],
        out_specs=pl.BlockSpec((seq, 2 * V_DIM), pair),
        out_shape=jax.ShapeDtypeStruct((t, N_HEADS * V_DIM), BF16),
        compiler_params=_params(("parallel", "parallel")),
        name="attn",
    )(q_nope, q_rope, k_nope, k_rope, v)


def _merge_kernel(x_ref, g_ref, wg_ref, bg_ref, pool_ref, wpp_ref, attn_ref, wap_ref, wout_ref, o_ref):
    tm, d = x_ref.shape
    ts = tm // MERGE_SUB_TILES
    for r0 in range(0, tm, ts):
        rows = slice(r0, r0 + ts)
        x = x_ref[rows, :]
        h = _rms(x, g_ref[...]).astype(BF16)
        gates = jax.nn.sigmoid(_dot_t(h, wg_ref[...]) + bg_ref[...])
        branch_a = _dot(pool_ref[rows, :], wpp_ref[...])
        branch_b = _dot(attn_ref[rows, :], wap_ref[...])
        merged = gates[:, :d] * branch_a + gates[:, d:] * branch_b
        o_ref[rows, :] = x + _dot(merged.astype(BF16), wout_ref[...])


def _merge(x, norm, w_gate, b_gate, pool, wpp, attn, wap, wout, layer):
    t, d = x.shape
    tm = min(MERGE_TOKEN_TILE, t)
    lay3 = lambda i: (layer, 0, 0)
    row = lambda i: (i, 0)
    whole = lambda i: (0, 0)
    resident = pl.Buffered(1)
    return pl.pallas_call(
        _merge_kernel,
        grid=(t // tm,),
        in_specs=[
            pl.BlockSpec((tm, d), row),
            pl.BlockSpec((None, 1, d), lay3),
            pl.BlockSpec(w_gate.shape, whole, pipeline_mode=resident),
            pl.BlockSpec((None, 1, 2 * d), lay3),
            pl.BlockSpec((tm, pool.shape[1]), row),
            pl.BlockSpec(wpp.shape, whole, pipeline_mode=resident),
            pl.BlockSpec((tm, attn.shape[1]), row),
            pl.BlockSpec(wap.shape, whole, pipeline_mode=resident),
            pl.BlockSpec(wout.shape, whole, pipeline_mode=resident),
        ],
        out_specs=pl.BlockSpec((tm, d), row),
        out_shape=jax.ShapeDtypeStruct((t, d), F32),
        compiler_params=_params(("parallel",)),
        name="merge",
    )(x, norm, w_gate, b_gate, pool, wpp, attn, wap, wout)


def _prep_w_kr(w_in_t):
    n_lat = POOL_DIM + 384 + 256
    kr = w_in_t[:, n_lat:n_lat + ROPE]
    return jnp.concatenate([kr, kr], axis=1).astype(BF16)


def _prep_w_uq(w_uq):
    l, r, _ = w_uq.shape
    w = w_uq.reshape(l, r, N_HEADS, QK_DIM)
    nope = w[..., :NOPE].reshape(l, r, N_HEADS * NOPE)
    rope = w[..., NOPE:]
    return jnp.concatenate([nope, rope.reshape(l, r, N_HEADS * ROPE)], axis=-1).astype(BF16)


def _prep_w_ukv(w_ukv):
    l, r, _ = w_ukv.shape
    w = w_ukv.reshape(l, r, N_HEADS, 2, NOPE)
    return jnp.swapaxes(w, 2, 3).reshape(l, r, 2 * N_HEADS * NOPE).astype(BF16)


def kernel(x, positions, norm_ffn1, ffn1_up, ffn1_down, norm_mix, w_in, b_gate, pool_maps, pool_scale, w_pool_proj,
           q_latent_norm, w_uq, kv_latent_norm, w_ukv, w_attn_proj, w_out, norm_ffn2, ffn2_up, ffn2_down, final_norm):
    batch, seq, d = x.shape
    depth = norm_ffn1.shape[0]
    t = batch * seq
    assert seq % 128 == 0 and d == 1024

    row3 = lambda a: a.reshape(a.shape[0], 1, a.shape[1])
    w_in_t = jnp.swapaxes(w_in, 1, 2)
    n_lat = POOL_DIM + 384 + 256
    w_kr = _prep_w_kr(w_in_t)
    wq = _prep_w_uq(w_uq)
    wkv = _prep_w_ukv(w_ukv)
    pmaps = pool_maps.astype(BF16)
    n1, nm, n2 = row3(norm_ffn1), row3(norm_mix), row3(norm_ffn2)
    qn, kvn, ps, bg = row3(q_latent_norm), row3(kv_latent_norm), row3(pool_scale), row3(b_gate)
    gf = final_norm.reshape(1, d)

    def whole(w, layer):
        return (w, layer, 0, w.shape[1])

    cos, sin = _rope_tables(positions)
    xt = x.reshape(t, d)
    w_ffn = (ffn1_up[0].astype(BF16), ffn1_down[0].astype(BF16))
    for layer in range(depth):
        casts = [whole(ffn2_up, layer), whole(ffn2_down, layer),
                 (w_in_t, layer, 0, n_lat), (w_in_t, layer, n_lat + ROPE, w_in_t.shape[1] - n_lat - ROPE),
                 whole(w_pool_proj, layer), whole(w_attn_proj, layer), whole(w_out, layer)]
        xt, (*w_ffn, w_lat, w_gate, wpp, wap, wout) = _ffn(xt, n1, layer, *w_ffn, casts=casts)
        pool, *qkv = _mix_in(xt, seq, nm, w_lat, w_kr, qn, wq, kvn, wkv, pmaps, ps, cos, sin, layer)
        attn = _attention(*qkv, batch, seq)
        xt = _merge(xt, nm, w_gate, bg, pool, wpp, attn, wap, wout, layer)
        if layer == depth - 1:
            xt, _ = _ffn(xt, n2, layer, *w_ffn, final_norm=gf)
        else:
            xt, w_ffn = _ffn(xt, n2, layer, *w_ffn, casts=[whole(ffn1_up, layer + 1), whole(ffn1_down, layer + 1)])
    return xt.reshape(batch, seq, d)
```

```python
import functools
import math

import jax
import jax.numpy as jnp
from jax import lax
from jax.experimental import pallas as pl
from jax.experimental.pallas import tpu as pltpu

F32 = jnp.float32
BF16 = jnp.bfloat16

N_HEADS = 8
NOPE = 128
ROPE = 64
HALF = ROPE // 2
V_DIM = 128
QK_DIM = NOPE + ROPE
ROPE_THETA = 10000.0
POOL_WINDOWS = (2, 4, 8, 16)
POOL_GROUP = 128
POOL_DIM = len(POOL_WINDOWS) * POOL_GROUP
POOL_CARRY = 32
Q_LATENT = 384
KV_LATENT = 256
N_LATENT = POOL_DIM + Q_LATENT + KV_LATENT
NORM_EPS = 1e-6
LANES = 128
BF16_SUBLANES = 16
NEG_BIG = -1e30

ROPE_TABLE_BLOCK = 4096
FFN_TOKEN_TILE = 1024
FFN_FF_TILE = 256
MIX_TOKEN_TILE = 512
MIX_SUB_TILES = 2
MERGE_TOKEN_TILE = 1024
MERGE_SUB_TILES = 2
ATTN_TILE = 512
VMEM_LIMIT = 56 * 1024 * 1024


def _rms(x, g):
    ms = jnp.mean(x * x, axis=-1, keepdims=True)
    return x * lax.rsqrt(ms + NORM_EPS) * g


def _dot(a, b):
    return jnp.dot(a, b, preferred_element_type=F32)


def _dot_t(a, b):
    return lax.dot_general(a, b, (((1,), (1,)), ((), ())), preferred_element_type=F32)


def _rotate_half(x, first_half):
    return jnp.where(first_half, -pltpu.roll(x, LANES - HALF, 1), pltpu.roll(x, HALF, 1))


def _params(sem):
    return pltpu.CompilerParams(dimension_semantics=sem, vmem_limit_bytes=VMEM_LIMIT)


def _rope_table_kernel(pos_ref, invf_ref, cos_ref, sin_ref):
    rows = pos_ref.shape[0]
    groups = LANES // HALF
    ang = pos_ref[...].astype(F32) * invf_ref[...]
    group = lax.broadcasted_iota(jnp.int32, (rows, LANES), 1) // HALF
    for table, out_ref in ((jnp.cos(ang), cos_ref), (jnp.sin(ang), sin_ref)):
        rolled = [table] + [pltpu.roll(table, HALF * k, 1) for k in range(1, groups)]
        for g in range(groups):
            spread = rolled[(-g) % groups]
            for k in range(1, groups):
                spread = jnp.where(group == k, rolled[(k - g) % groups], spread)
            out_ref[g * rows:(g + 1) * rows, :] = spread


def _rope_tables(positions):
    t = positions.size
    inv_freq = ROPE_THETA ** (-jnp.arange(0, ROPE, 2, dtype=F32) / ROPE)
    groups = LANES // HALF
    block = min(t, ROPE_TABLE_BLOCK)
    rows = block // groups
    pos = positions.reshape(t // block, groups, rows)
    pos = jnp.repeat(jnp.swapaxes(pos, 1, 2), HALF, axis=2).reshape(t // groups, LANES)
    invf = jnp.tile(inv_freq, groups).reshape(1, LANES)
    return pl.pallas_call(
        _rope_table_kernel,
        grid=(t // block,),
        in_specs=[pl.BlockSpec((rows, LANES), lambda i: (i, 0)),
                  pl.BlockSpec((1, LANES), lambda i: (0, 0))],
        out_specs=[pl.BlockSpec((block, LANES), lambda i: (i, 0))] * 2,
        out_shape=[jax.ShapeDtypeStruct((t, LANES), F32)] * 2,
        compiler_params=_params(("parallel",)),
        name="rope_tables",
    )(pos, invf)


def _ffn_kernel(*refs, d_ff, tf, final, n_casts):
    x_ref, g_ref, wup_ref, wd_ref = refs[:4]
    if final:
        gf_ref, o_ref = refs[4:]
    else:
        o_ref = refs[4 + n_casts]
        for src, dst in zip(refs[4:4 + n_casts], refs[5 + n_casts:]):
            dst[...] = src[...].astype(BF16)
    x = x_ref[...]
    h = _rms(x, g_ref[...]).astype(BF16)
    acc = None
    for lo in range(0, d_ff, tf):
        gate = _dot(h, wup_ref[:, lo:lo + tf])
        up = _dot(h, wup_ref[:, d_ff + lo:d_ff + lo + tf])
        act = (gate * jax.nn.sigmoid(gate) * up).astype(BF16)
        part = _dot(act, wd_ref[lo:lo + tf, :])
        acc = part if acc is None else acc + part
    y = x + 0.5 * acc
    if final:
        y = _rms(y, gf_ref[...])
    o_ref[...] = y


def _ffn(x, norm, layer, w_up, w_down, casts=(), final_norm=None):
    t, d = x.shape
    f = w_down.shape[0]
    tm = min(FFN_TOKEN_TILE, t)
    steps = t // tm
    final = final_norm is not None
    resident = pl.Buffered(1)
    in_specs = [
        pl.BlockSpec((tm, d), lambda i: (i, 0)),
        pl.BlockSpec((None, 1, d), lambda i: (layer, 0, 0)),
        pl.BlockSpec((d, 2 * f), lambda i: (0, 0), pipeline_mode=resident),
        pl.BlockSpec((f, d), lambda i: (0, 0), pipeline_mode=resident),
    ]
    args = [x, norm, w_up, w_down]
    out_specs = [pl.BlockSpec((tm, d), lambda i: (i, 0))]
    out_shape = [jax.ShapeDtypeStruct((t, d), F32)]
    if final:
        assert not casts
        in_specs.append(pl.BlockSpec((1, d), lambda i: (0, 0)))
        args.append(final_norm)
    in_kernel = []
    for job, (src, src_layer, first_row, n_rows) in enumerate(casts):
        slabs = math.gcd(steps, n_rows // BF16_SUBLANES)
        slab_rows = n_rows // slabs
        if n_rows % (slabs * BF16_SUBLANES) or first_row % slab_rows:
            continue
        in_kernel.append(job)
        width = src.shape[2]
        in_specs.append(pl.BlockSpec(
            (None, slab_rows, width),
            lambda i, sl=src_layer, fs=first_row // slab_rows, rep=steps // slabs: (sl, fs + i // rep, 0)))
        args.append(src)
        out_specs.append(pl.BlockSpec((slab_rows, width), lambda i, rep=steps // slabs: (i // rep, 0)))
        out_shape.append(jax.ShapeDtypeStruct((n_rows, width), BF16))
    out = pl.pallas_call(
        functools.partial(_ffn_kernel, d_ff=f, tf=FFN_FF_TILE, final=final, n_casts=len(in_kernel)),
        grid=(steps,),
        in_specs=in_specs,
        out_specs=out_specs,
        out_shape=out_shape,
        compiler_params=_params(("arbitrary",)),
        name="ffn_final" if final else "ffn",
    )(*args)
    done = dict(zip(in_kernel, out[1:]))
    cast = [done[job] if job in done else src[sl, first:first + n].astype(BF16)
            for job, (src, sl, first, n) in enumerate(casts)]
    return out[0], cast


def _mix_in_kernel(x_ref, g_ref, wl_ref, wkr_ref, qn_ref, wq_ref, kvn_ref, wkv_ref, pm_ref, ps_ref, cos_ref, sin_ref,
                   pool_ref, qnope_ref, qrope_ref, knope_ref, krope_ref, v_ref, e_ref, t1_ref, t2_ref, t3_ref,
                   *, tm, tiles_per_seq):
    c = POOL_CARRY
    g = POOL_GROUP
    seq_tile = lax.rem(pl.program_id(0), tiles_per_seq)

    @pl.when(seq_tile == 0)
    def _():
        e_ref[0:c, :] = jnp.zeros((c, POOL_DIM), F32)

    ts = tm // MIX_SUB_TILES
    lane = lax.broadcasted_iota(jnp.int32, (ts, LANES), 1)
    first_half = (lane & (ROPE - 1)) < HALF
    n_all = N_HEADS * NOPE
    for r0 in range(0, tm, ts):
        rows = slice(r0, r0 + ts)
        h = _rms(x_ref[rows, :], g_ref[...]).astype(BF16)
        p = _dot_t(h, wl_ref[...])
        kr_main = _dot_t(h, wkr_ref[...])
        e_ref[c + r0:c + r0 + ts, :] = p[:, :POOL_DIM]
        q_lat = p[:, POOL_DIM:POOL_DIM + Q_LATENT]
        kv_lat = p[:, POOL_DIM + Q_LATENT:N_LATENT]
        cos = cos_ref[rows, :]
        sin = sin_ref[rows, :]

        qa = _dot(_rms(q_lat, qn_ref[...]).astype(BF16), wq_ref[...])
        qnope_ref[rows, :] = qa[:, :n_all].astype(BF16)
        for pair in range(N_HEADS // 2):
            main = qa[:, n_all + pair * LANES:n_all + (pair + 1) * LANES]
            roped = main * cos + _rotate_half(main, first_half) * sin
            qrope_ref[rows, pair * LANES:(pair + 1) * LANES] = roped.astype(BF16)

        krope_ref[rows, :] = (kr_main * cos + _rotate_half(kr_main, first_half) * sin).astype(BF16)
        kv = _dot(_rms(kv_lat, kvn_ref[...]).astype(BF16), wkv_ref[...])
        knope_ref[rows, :] = kv[:, :n_all].astype(BF16)
        v_ref[rows, :] = kv[:, n_all:].astype(BF16)

    xp = e_ref[c:c + tm, :]
    t1_ref[8:c + tm, :] = e_ref[8:c + tm, :] + e_ref[7:c + tm - 1, :]
    t2_ref[16:c + tm, :] = t1_ref[16:c + tm, g:4 * g] + t1_ref[14:c + tm - 2, g:4 * g]
    t3_ref[24:c + tm, :] = t2_ref[24:c + tm, g:3 * g] + t2_ref[20:c + tm - 4, g:3 * g]
    w16 = t3_ref[c:c + tm, g:2 * g] + t3_ref[c - 8:c + tm - 8, g:2 * g]
    sums = (t1_ref[c:c + tm, 0:g], t2_ref[c:c + tm, 0:g], t3_ref[c:c + tm, 0:g], w16)
    e_ref[0:c, :] = e_ref[tm:tm + c, :]

    pos1 = (seq_tile * tm + 1 + lax.broadcasted_iota(jnp.int32, (tm, 1), 0)).astype(F32)
    for gi, w in enumerate(POOL_WINDOWS):
        count = jnp.minimum(pos1, float(w))
        pooled = sums[gi] / count - xp[:, gi * g:(gi + 1) * g]
        mixed = _dot(pooled.astype(BF16), pm_ref[gi]) * ps_ref[:, gi * g:(gi + 1) * g]
        pool_ref[:, gi * g:(gi + 1) * g] = mixed.astype(BF16)


def _mix_in(x, seq, norm, w_lat, w_kr, qn, wq, kvn, wkv, pmaps, pscale, cos, sin, layer):
    t, d = x.shape
    tm = min(MIX_TOKEN_TILE, seq)
    tiles_per_seq = seq // tm
    c = POOL_CARRY
    lay3 = lambda i: (layer, 0, 0)
    row = lambda i: (i, 0)
    in_specs = [
        pl.BlockSpec((tm, d), row),
        pl.BlockSpec((None, 1, d), lay3),
        pl.BlockSpec(w_lat.shape, lambda i: (0, 0)),
        pl.BlockSpec((None,) + w_kr.shape[1:], lay3),
        pl.BlockSpec((None, 1, qn.shape[2]), lay3),
        pl.BlockSpec((None,) + wq.shape[1:], lay3),
        pl.BlockSpec((None, 1, kvn.shape[2]), lay3),
        pl.BlockSpec((None,) + wkv.shape[1:], lay3),
        pl.BlockSpec((None,) + pmaps.shape[1:], lambda i: (layer, 0, 0, 0)),
        pl.BlockSpec((None, 1, POOL_DIM), lay3),
        pl.BlockSpec((tm, LANES), row),
        pl.BlockSpec((tm, LANES), row),
    ]
    widths = (POOL_DIM, N_HEADS * NOPE, N_HEADS * ROPE, N_HEADS * NOPE, LANES, N_HEADS * V_DIM)
    out_shape = [jax.ShapeDtypeStruct((t, w), BF16) for w in widths]
    out_specs = [pl.BlockSpec((tm, w), row) for w in widths]
    g = POOL_GROUP
    return pl.pallas_call(
        functools.partial(_mix_in_kernel, tm=tm, tiles_per_seq=tiles_per_seq),
        grid=(t // tm,),
        in_specs=in_specs,
        out_specs=out_specs,
        out_shape=out_shape,
        scratch_shapes=[pltpu.VMEM((c + tm, 4 * g), F32), pltpu.VMEM((c + tm, 4 * g), F32),
                        pltpu.VMEM((c + tm, 3 * g), F32), pltpu.VMEM((c + tm, 2 * g), F32)],
        compiler_params=_params(("arbitrary",)),
        name="mix_in",
    )(x, norm, w_lat, w_kr, qn, wq, kvn, wkv, pmaps, pscale, cos, sin)


def _attn_kernel(qnope_ref, qrope_ref, knope_ref, krope_ref, v_ref, o_ref, *, seq, tile, exp2_scale):
    tq = tk = tile
    causal = lax.broadcasted_iota(jnp.int32, (tq, tk), 0) >= lax.broadcasted_iota(jnp.int32, (tq, tk), 1)
    lane = lax.broadcasted_iota(jnp.int32, (tq, LANES), 1)
    for qi in range(seq // tq):
        q0 = qi * tq
        q_pair = qrope_ref[q0:q0 + tq, :]
        for hh in range(2):
            cols = slice(hh * NOPE, (hh + 1) * NOPE)
            own = (lane < ROPE) if hh == 0 else (lane >= ROPE)
            q = jnp.concatenate([qnope_ref[q0:q0 + tq, cols], jnp.where(own, q_pair, jnp.zeros_like(q_pair))], axis=1)
            m = l = acc = None
            for k0 in range(0, q0 + tq, tk):
                k = jnp.concatenate([knope_ref[k0:k0 + tk, cols], krope_ref[k0:k0 + tk, :]], axis=1)
                v = v_ref[k0:k0 + tk, cols]
                s = _dot_t(q, k)
                if k0 == q0:
                    s = jnp.where(causal, s, NEG_BIG)
                s_max = jnp.max(s, axis=-1, keepdims=True)
                if k0 == 0:
                    m = s_max
                    p = jnp.exp2((s - m) * exp2_scale)
                    l = jnp.sum(p, axis=-1, keepdims=True)
                    acc = _dot(p.astype(BF16), v)
                else:
                    m_new = jnp.maximum(m, s_max)
                    alpha = jnp.exp2((m - m_new) * exp2_scale)
                    p = jnp.exp2((s - m_new) * exp2_scale)
                    l = alpha * l + jnp.sum(p, axis=-1, keepdims=True)
                    acc = alpha * acc + _dot(p.astype(BF16), v)
                    m = m_new
            o_ref[q0:q0 + tq, cols] = (acc / l).astype(BF16)


def _attention(q_nope, q_rope, k_nope, k_rope, v, batch, seq):
    t = v.shape[0]
    tile = min(ATTN_TILE, seq)
    exp2_scale = (QK_DIM ** -0.5) * math.log2(math.e)
    pair = lambda b, h: (b, h)
    return pl.pallas_call(
        functools.partial(_attn_kernel, seq=seq, tile=tile, exp2_scale=exp2_scale),
        grid=(batch, N_HEADS // 2),
        in_specs=[pl.BlockSpec((seq, 2 * NOPE), pair),
                  pl.BlockSpec((seq, LANES), pair),
                  pl.BlockSpec((seq, 2 * NOPE), pair),
                  pl.BlockSpec((seq, LANES), lambda b, h: (b, 0)),
                  pl.BlockSpec((seq, 2 * V_DIM), pair)],
        out_specs=pl.BlockSpec((seq, 2 * V_DIM), pair),
        out_shape=jax.ShapeDtypeStruct((t, N_HEADS * V_DIM), BF16),
        compiler_params=_params(("parallel", "parallel")),
        name="attn",
    )(q_nope, q_rope, k_nope, k_rope, v)


def _merge_kernel(x_ref, g_ref, wg_ref, bg_ref, pool_ref, wpp_ref, attn_ref, wap_ref, wout_ref, o_ref):
    tm, d = x_ref.shape
    ts = tm // MERGE_SUB_TILES
    for r0 in range(0, tm, ts):
        rows = slice(r0, r0 + ts)
        x = x_ref[rows, :]
        h = _rms(x, g_ref[...]).astype(BF16)
        gates = jax.nn.sigmoid(_dot_t(h, wg_ref[...]) + bg_ref[...])
        branch_a = _dot(pool_ref[rows, :], wpp_ref[...])
        branch_b = _dot(attn_ref[rows, :], wap_ref[...])
        merged = gates[:, :d] * branch_a + gates[:, d:] * branch_b
        o_ref[rows, :] = x + _dot(merged.astype(BF16), wout_ref[...])


def _merge(x, norm, w_gate, b_gate, pool, wpp, attn, wap, wout, layer):
    t, d = x.shape
    tm = min(MERGE_TOKEN_TILE, t)
    lay3 = lambda i: (layer, 0, 0)
    row = lambda i: (i, 0)
    whole = lambda i: (0, 0)
    resident = pl.Buffered(1)
    return pl.pallas_call(
        _merge_kernel,
        grid=(t // tm,),
        in_specs=[
            pl.BlockSpec((tm, d), row),
            pl.BlockSpec((None, 1, d), lay3),
            pl.BlockSpec(w_gate.shape, whole, pipeline_mode=resident),
            pl.BlockSpec((None, 1, 2 * d), lay3),
            pl.BlockSpec((tm, pool.shape[1]), row),
            pl.BlockSpec(wpp.shape, whole, pipeline_mode=resident),
            pl.BlockSpec((tm, attn.shape[1]), row),
            pl.BlockSpec(wap.shape, whole, pipeline_mode=resident),
            pl.BlockSpec(wout.shape, whole, pipeline_mode=resident),
        ],
        out_specs=pl.BlockSpec((tm, d), row),
        out_shape=jax.ShapeDtypeStruct((t, d), F32),
        compiler_params=_params(("parallel",)),
        name="merge",
    )(x, norm, w_gate, b_gate, pool, wpp, attn, wap, wout)


def _prep_w_kr(w_in_t):
    kr = w_in_t[:, N_LATENT:N_LATENT + ROPE]
    return jnp.concatenate([kr, kr], axis=1).astype(BF16)


def _prep_w_uq(w_uq):
    l, r, _ = w_uq.shape
    w = w_uq.reshape(l, r, N_HEADS, QK_DIM)
    nope = w[..., :NOPE].reshape(l, r, N_HEADS * NOPE)
    rope = w[..., NOPE:]
    return jnp.concatenate([nope, rope.reshape(l, r, N_HEADS * ROPE)], axis=-1).astype(BF16)


def _prep_w_ukv(w_ukv):
    l, r, _ = w_ukv.shape
    w = w_ukv.reshape(l, r, N_HEADS, 2, NOPE)
    return jnp.swapaxes(w, 2, 3).reshape(l, r, 2 * N_HEADS * NOPE).astype(BF16)


def kernel(x, positions, norm_ffn1, ffn1_up, ffn1_down, norm_mix, w_in, b_gate, pool_maps, pool_scale, w_pool_proj,
           q_latent_norm, w_uq, kv_latent_norm, w_ukv, w_attn_proj, w_out, norm_ffn2, ffn2_up, ffn2_down, final_norm):
    batch, seq, d = x.shape
    depth = norm_ffn1.shape[0]
    t = batch * seq
    assert seq % 128 == 0 and d == 1024

    row3 = lambda a: a.reshape(a.shape[0], 1, a.shape[1])
    w_in_t = jnp.swapaxes(w_in, 1, 2)
    n_gate = w_in_t.shape[1] - N_LATENT - ROPE
    w_kr = _prep_w_kr(w_in_t)
    wq = _prep_w_uq(w_uq)
    wkv = _prep_w_ukv(w_ukv)
    pmaps = pool_maps.astype(BF16)
    n1, nm, n2 = row3(norm_ffn1), row3(norm_mix), row3(norm_ffn2)
    qn, kvn, ps, bg = row3(q_latent_norm), row3(kv_latent_norm), row3(pool_scale), row3(b_gate)
    gf = final_norm.reshape(1, d)

    def whole(w, layer):
        return (w, layer, 0, w.shape[1])

    cos, sin = _rope_tables(positions)
    xt = x.reshape(t, d)
    w_ffn = (ffn1_up[0].astype(BF16), ffn1_down[0].astype(BF16))
    for layer in range(depth):
        casts = [whole(ffn2_up, layer), whole(ffn2_down, layer),
                 (w_in_t, layer, 0, N_LATENT), (w_in_t, layer, N_LATENT + ROPE, n_gate),
                 whole(w_pool_proj, layer), whole(w_attn_proj, layer), whole(w_out, layer)]
        xt, (*w_ffn, w_lat, w_gate, wpp, wap, wout) = _ffn(xt, n1, layer, *w_ffn, casts=casts)
        pool, *qkv = _mix_in(xt, seq, nm, w_lat, w_kr, qn, wq, kvn, wkv, pmaps, ps, cos, sin, layer)
        attn = _attention(*qkv, batch, seq)
        xt = _merge(xt, nm, w_gate, bg, pool, wpp, attn, wap, wout, layer)
        if layer == depth - 1:
            xt, _ = _ffn(xt, n2, layer, *w_ffn, final_norm=gf)
        else:
            xt, w_ffn = _ffn(xt, n2, layer, *w_ffn, casts=[whole(ffn1_up, layer + 1), whole(ffn1_down, layer + 1)])
    return xt.reshape(batch, seq, d)
```

```python
import functools
import math

import jax
import jax.numpy as jnp
from jax import lax
from jax.experimental import pallas as pl
from jax.experimental.pallas import tpu as pltpu

F32 = jnp.float32
BF16 = jnp.bfloat16

N_HEADS = 8
NOPE = 128
ROPE = 64
HALF = ROPE // 2
V_DIM = 128
QK_DIM = NOPE + ROPE
EXP2_SCALE = QK_DIM ** -0.5 * math.log2(math.e)
ROPE_THETA = 10000.0
POOL_WINDOWS = (2, 4, 8, 16)
POOL_GROUP = 128
POOL_DIM = len(POOL_WINDOWS) * POOL_GROUP
POOL_CARRY = 32
Q_LATENT = 384
KV_LATENT = 256
N_LATENT = POOL_DIM + Q_LATENT + KV_LATENT
NORM_EPS = 1e-6
LANES = 128
BF16_SUBLANES = 16
NEG_BIG = -1e30

ROPE_TABLE_BLOCK = 4096
FFN_TOKEN_TILE = 1024
FFN_FF_TILE = 256
MIX_TOKEN_TILE = 512
MIX_SUB_TILES = 2
MERGE_TOKEN_TILE = 1024
MERGE_SUB_TILES = 2
ATTN_TILE = 512
VMEM_LIMIT = 56 * 1024 * 1024


def _rms(x, g):
    ms = jnp.mean(x * x, axis=-1, keepdims=True)
    return x * lax.rsqrt(ms + NORM_EPS) * g


def _dot(a, b):
    return jnp.dot(a, b, preferred_element_type=F32)


def _dot_t(a, b):
    return lax.dot_general(a, b, (((1,), (1,)), ((), ())), preferred_element_type=F32)


def _rotate_half(x, first_half):
    return jnp.where(first_half, -pltpu.roll(x, LANES - HALF, 1), pltpu.roll(x, HALF, 1))


def _params(sem):
    return pltpu.CompilerParams(dimension_semantics=sem, vmem_limit_bytes=VMEM_LIMIT)


def _rope_table_kernel(pos_ref, invf_ref, cos_ref, sin_ref):
    rows = pos_ref.shape[0]
    groups = LANES // HALF
    ang = pos_ref[...].astype(F32) * invf_ref[...]
    group = lax.broadcasted_iota(jnp.int32, (rows, LANES), 1) // HALF
    for table, out_ref in ((jnp.cos(ang), cos_ref), (jnp.sin(ang), sin_ref)):
        rolled = [table] + [pltpu.roll(table, HALF * k, 1) for k in range(1, groups)]
        for g in range(groups):
            spread = rolled[(-g) % groups]
            for k in range(1, groups):
                spread = jnp.where(group == k, rolled[(k - g) % groups], spread)
            out_ref[g * rows:(g + 1) * rows, :] = spread


def _rope_tables(positions):
    t = positions.size
    inv_freq = ROPE_THETA ** (-jnp.arange(0, ROPE, 2, dtype=F32) / ROPE)
    groups = LANES // HALF
    block = min(t, ROPE_TABLE_BLOCK)
    rows = block // groups
    pos = positions.reshape(t // block, groups, rows)
    pos = jnp.repeat(jnp.swapaxes(pos, 1, 2), HALF, axis=2).reshape(t // groups, LANES)
    invf = jnp.tile(inv_freq, groups).reshape(1, LANES)
    return pl.pallas_call(
        _rope_table_kernel,
        grid=(t // block,),
        in_specs=[pl.BlockSpec((rows, LANES), lambda i: (i, 0)),
                  pl.BlockSpec((1, LANES), lambda i: (0, 0))],
        out_specs=[pl.BlockSpec((block, LANES), lambda i: (i, 0))] * 2,
        out_shape=[jax.ShapeDtypeStruct((t, LANES), F32)] * 2,
        compiler_params=_params(("parallel",)),
        name="rope_tables",
    )(pos, invf)


def _ffn_kernel(*refs, d_ff, tf, final, n_casts):
    x_ref, g_ref, wup_ref, wd_ref = refs[:4]
    if final:
        gf_ref, o_ref = refs[4:]
    else:
        o_ref = refs[4 + n_casts]
        for src, dst in zip(refs[4:4 + n_casts], refs[5 + n_casts:]):
            dst[...] = src[...].astype(BF16)
    x = x_ref[...]
    h = _rms(x, g_ref[...]).astype(BF16)
    acc = None
    for lo in range(0, d_ff, tf):
        gate = _dot(h, wup_ref[:, lo:lo + tf])
        up = _dot(h, wup_ref[:, d_ff + lo:d_ff + lo + tf])
        act = (gate * jax.nn.sigmoid(gate) * up).astype(BF16)
        part = _dot(act, wd_ref[lo:lo + tf, :])
        acc = part if acc is None else acc + part
    y = x + 0.5 * acc
    if final:
        y = _rms(y, gf_ref[...])
    o_ref[...] = y


def _ffn(x, norm, layer, w_up, w_down, casts=(), final_norm=None):
    t, d = x.shape
    f = w_down.shape[0]
    tm = min(FFN_TOKEN_TILE, t)
    steps = t // tm
    final = final_norm is not None
    resident = pl.Buffered(1)
    in_specs = [
        pl.BlockSpec((tm, d), lambda i: (i, 0)),
        pl.BlockSpec((None, 1, d), lambda i: (layer, 0, 0)),
        pl.BlockSpec((d, 2 * f), lambda i: (0, 0), pipeline_mode=resident),
        pl.BlockSpec((f, d), lambda i: (0, 0), pipeline_mode=resident),
    ]
    args = [x, norm, w_up, w_down]
    out_specs = [pl.BlockSpec((tm, d), lambda i: (i, 0))]
    out_shape = [jax.ShapeDtypeStruct((t, d), F32)]
    if final:
        assert not casts
        in_specs.append(pl.BlockSpec((1, d), lambda i: (0, 0)))
        args.append(final_norm)
    in_kernel = []
    for job, (src, src_layer, first_row, n_rows) in enumerate(casts):
        slabs = math.gcd(steps, n_rows // BF16_SUBLANES)
        slab_rows = n_rows // slabs
        if n_rows % (slabs * BF16_SUBLANES) or first_row % slab_rows:
            continue
        in_kernel.append(job)
        width = src.shape[2]
        in_specs.append(pl.BlockSpec(
            (None, slab_rows, width),
            lambda i, sl=src_layer, fs=first_row // slab_rows, rep=steps // slabs: (sl, fs + i // rep, 0)))
        args.append(src)
        out_specs.append(pl.BlockSpec((slab_rows, width), lambda i, rep=steps // slabs: (i // rep, 0)))
        out_shape.append(jax.ShapeDtypeStruct((n_rows, width), BF16))
    out = pl.pallas_call(
        functools.partial(_ffn_kernel, d_ff=f, tf=FFN_FF_TILE, final=final, n_casts=len(in_kernel)),
        grid=(steps,),
        in_specs=in_specs,
        out_specs=out_specs,
        out_shape=out_shape,
        compiler_params=_params(("arbitrary",)),
        name="ffn_final" if final else "ffn",
    )(*args)
    done = dict(zip(in_kernel, out[1:]))
    cast = [done[job] if job in done else src[sl, first:first + n].astype(BF16)
            for job, (src, sl, first, n) in enumerate(casts)]
    return out[0], cast


def _mix_in_kernel(x_ref, g_ref, wl_ref, wkr_ref, qn_ref, wq_ref, kvn_ref, wkv_ref, pm_ref, ps_ref, cos_ref, sin_ref,
                   pool_ref, qnope_ref, qrope_ref, knope_ref, krope_ref, v_ref, e_ref, t1_ref, t2_ref, t3_ref,
                   *, tm, tiles_per_seq):
    c = POOL_CARRY
    g = POOL_GROUP
    seq_tile = lax.rem(pl.program_id(0), tiles_per_seq)

    @pl.when(seq_tile == 0)
    def _():
        e_ref[0:c, :] = jnp.zeros((c, POOL_DIM), F32)

    ts = tm // MIX_SUB_TILES
    lane = lax.broadcasted_iota(jnp.int32, (ts, LANES), 1)
    first_half = (lane & (ROPE - 1)) < HALF
    n_all = N_HEADS * NOPE
    for r0 in range(0, tm, ts):
        rows = slice(r0, r0 + ts)
        h = _rms(x_ref[rows, :], g_ref[...]).astype(BF16)
        p = _dot_t(h, wl_ref[...])
        kr_main = _dot_t(h, wkr_ref[...])
        e_ref[c + r0:c + r0 + ts, :] = p[:, :POOL_DIM]
        q_lat = p[:, POOL_DIM:POOL_DIM + Q_LATENT]
        kv_lat = p[:, POOL_DIM + Q_LATENT:N_LATENT]
        cos = cos_ref[rows, :]
        sin = sin_ref[rows, :]

        qa = _dot(_rms(q_lat, qn_ref[...]).astype(BF16), wq_ref[...]) * EXP2_SCALE
        qnope_ref[rows, :] = qa[:, :n_all].astype(BF16)
        for pair in range(N_HEADS // 2):
            main = qa[:, n_all + pair * LANES:n_all + (pair + 1) * LANES]
            roped = main * cos + _rotate_half(main, first_half) * sin
            qrope_ref[rows, pair * LANES:(pair + 1) * LANES] = roped.astype(BF16)

        krope_ref[rows, :] = (kr_main * cos + _rotate_half(kr_main, first_half) * sin).astype(BF16)
        kv = _dot(_rms(kv_lat, kvn_ref[...]).astype(BF16), wkv_ref[...])
        knope_ref[rows, :] = kv[:, :n_all].astype(BF16)
        v_ref[rows, :] = kv[:, n_all:].astype(BF16)

    xp = e_ref[c:c + tm, :]
    t1_ref[8:c + tm, :] = e_ref[8:c + tm, :] + e_ref[7:c + tm - 1, :]
    t2_ref[16:c + tm, :] = t1_ref[16:c + tm, g:4 * g] + t1_ref[14:c + tm - 2, g:4 * g]
    t3_ref[24:c + tm, :] = t2_ref[24:c + tm, g:3 * g] + t2_ref[20:c + tm - 4, g:3 * g]
    w16 = t3_ref[c:c + tm, g:2 * g] + t3_ref[c - 8:c + tm - 8, g:2 * g]
    sums = (t1_ref[c:c + tm, 0:g], t2_ref[c:c + tm, 0:g], t3_ref[c:c + tm, 0:g], w16)
    e_ref[0:c, :] = e_ref[tm:tm + c, :]

    pos1 = (seq_tile * tm + 1 + lax.broadcasted_iota(jnp.int32, (tm, 1), 0)).astype(F32)
    for gi, w in enumerate(POOL_WINDOWS):
        count = jnp.minimum(pos1, float(w))
        pooled = sums[gi] / count - xp[:, gi * g:(gi + 1) * g]
        mixed = _dot(pooled.astype(BF16), pm_ref[gi]) * ps_ref[:, gi * g:(gi + 1) * g]
        pool_ref[:, gi * g:(gi + 1) * g] = mixed.astype(BF16)


def _mix_in(x, seq, norm, w_lat, w_kr, qn, wq, kvn, wkv, pmaps, pscale, cos, sin, layer):
    t, d = x.shape
    tm = min(MIX_TOKEN_TILE, seq)
    tiles_per_seq = seq // tm
    c = POOL_CARRY
    lay3 = lambda i: (layer, 0, 0)
    row = lambda i: (i, 0)
    in_specs = [
        pl.BlockSpec((tm, d), row),
        pl.BlockSpec((None, 1, d), lay3),
        pl.BlockSpec(w_lat.shape, lambda i: (0, 0)),
        pl.BlockSpec((None,) + w_kr.shape[1:], lay3),
        pl.BlockSpec((None, 1, qn.shape[2]), lay3),
        pl.BlockSpec((None,) + wq.shape[1:], lay3),
        pl.BlockSpec((None, 1, kvn.shape[2]), lay3),
        pl.BlockSpec((None,) + wkv.shape[1:], lay3),
        pl.BlockSpec((None,) + pmaps.shape[1:], lambda i: (layer, 0, 0, 0)),
        pl.BlockSpec((None, 1, POOL_DIM), lay3),
        pl.BlockSpec((tm, LANES), row),
        pl.BlockSpec((tm, LANES), row),
    ]
    widths = (POOL_DIM, N_HEADS * NOPE, N_HEADS * ROPE, N_HEADS * NOPE, LANES, N_HEADS * V_DIM)
    out_shape = [jax.ShapeDtypeStruct((t, w), BF16) for w in widths]
    out_specs = [pl.BlockSpec((tm, w), row) for w in widths]
    g = POOL_GROUP
    return pl.pallas_call(
        functools.partial(_mix_in_kernel, tm=tm, tiles_per_seq=tiles_per_seq),
        grid=(t // tm,),
        in_specs=in_specs,
        out_specs=out_specs,
        out_shape=out_shape,
        scratch_shapes=[pltpu.VMEM((c + tm, 4 * g), F32), pltpu.VMEM((c + tm, 4 * g), F32),
                        pltpu.VMEM((c + tm, 3 * g), F32), pltpu.VMEM((c + tm, 2 * g), F32)],
        compiler_params=_params(("arbitrary",)),
        name="mix_in",
    )(x, norm, w_lat, w_kr, qn, wq, kvn, wkv, pmaps, pscale, cos, sin)


def _attn_kernel(qnope_ref, qrope_ref, knope_ref, krope_ref, v_ref, o_ref, *, seq, tile):
    tq = tk = tile
    causal = lax.broadcasted_iota(jnp.int32, (tq, tk), 0) >= lax.broadcasted_iota(jnp.int32, (tq, tk), 1)
    lane = lax.broadcasted_iota(jnp.int32, (tq, LANES), 1)
    for qi in range(seq // tq):
        q0 = qi * tq
        q_pair = qrope_ref[q0:q0 + tq, :]
        for hh in range(2):
            cols = slice(hh * NOPE, (hh + 1) * NOPE)
            own = (lane < ROPE) if hh == 0 else (lane >= ROPE)
            q = jnp.concatenate([qnope_ref[q0:q0 + tq, cols], jnp.where(own, q_pair, jnp.zeros_like(q_pair))], axis=1)
            m = l = acc = None
            for k0 in range(0, q0 + tq, tk):
                k = jnp.concatenate([knope_ref[k0:k0 + tk, cols], krope_ref[k0:k0 + tk, :]], axis=1)
                v = v_ref[k0:k0 + tk, cols]
                s = _dot_t(q, k)
                if k0 == q0:
                    s = jnp.where(causal, s, NEG_BIG)
                s_max = jnp.max(s, axis=-1, keepdims=True)
                if k0 == 0:
                    m = s_max
                    p = jnp.exp2(s - m)
                    l = jnp.sum(p, axis=-1, keepdims=True)
                    acc = _dot(p.astype(BF16), v)
                else:
                    m_new = jnp.maximum(m, s_max)
                    alpha = jnp.exp2(m - m_new)
                    p = jnp.exp2(s - m_new)
                    l = alpha * l + jnp.sum(p, axis=-1, keepdims=True)
                    acc = alpha * acc + _dot(p.astype(BF16), v)
                    m = m_new
            o_ref[q0:q0 + tq, cols] = (acc / l).astype(BF16)


def _attention(q_nope, q_rope, k_nope, k_rope, v, batch, seq):
    t = v.shape[0]
    tile = min(ATTN_TILE, seq)
    pair = lambda b, h: (b, h)
    return pl.pallas_call(
        functools.partial(_attn_kernel, seq=seq, tile=tile),
        grid=(batch, N_HEADS // 2),
        in_specs=[pl.BlockSpec((seq, 2 * NOPE), pair),
                  pl.BlockSpec((seq, LANES), pair),
                  pl.BlockSpec((seq, 2 * NOPE), pair),
                  pl.BlockSpec((seq, LANES), lambda b, h: (b, 0)),
                  pl.BlockSpec((seq, 2 * V_DIM), pair)],
        out_specs=pl.BlockSpec((seq, 2 * V_DIM), pair),
        out_shape=jax.ShapeDtypeStruct((t, N_HEADS * V_DIM), BF16),
        compiler_params=_params(("parallel", "parallel")),
        name="attn",
    )(q_nope, q_rope, k_nope, k_rope, v)


def _merge_kernel(x_ref, g_ref, wg_ref, bg_ref, pool_ref, wpp_ref, attn_ref, wap_ref, wout_ref, o_ref):
    tm, d = x_ref.shape
    ts = tm // MERGE_SUB_TILES
    for r0 in range(0, tm, ts):
        rows = slice(r0, r0 + ts)
        x = x_ref[rows, :]
        h = _rms(x, g_ref[...]).astype(BF16)
        gates = jax.nn.sigmoid(_dot_t(h, wg_ref[...]) + bg_ref[...])
        branch_a = _dot(pool_ref[rows, :], wpp_ref[...])
        branch_b = _dot(attn_ref[rows, :], wap_ref[...])
        merged = gates[:, :d] * branch_a + gates[:, d:] * branch_b
        o_ref[rows, :] = x + _dot(merged.astype(BF16), wout_ref[...])


def _merge(x, norm, w_gate, b_gate, pool, wpp, attn, wap, wout, layer):
    t, d = x.shape
    tm = min(MERGE_TOKEN_TILE, t)
    lay3 = lambda i: (layer, 0, 0)
    row = lambda i: (i, 0)
    whole = lambda i: (0, 0)
    resident = pl.Buffered(1)
    return pl.pallas_call(
        _merge_kernel,
        grid=(t // tm,),
        in_specs=[
            pl.BlockSpec((tm, d), row),
            pl.BlockSpec((None, 1, d), lay3),
            pl.BlockSpec(w_gate.shape, whole, pipeline_mode=resident),
            pl.BlockSpec((None, 1, 2 * d), lay3),
            pl.BlockSpec((tm, pool.shape[1]), row),
            pl.BlockSpec(wpp.shape, whole, pipeline_mode=resident),
            pl.BlockSpec((tm, attn.shape[1]), row),
            pl.BlockSpec(wap.shape, whole, pipeline_mode=resident),
            pl.BlockSpec(wout.shape, whole, pipeline_mode=resident),
        ],
        out_specs=pl.BlockSpec((tm, d), row),
        out_shape=jax.ShapeDtypeStruct((t, d), F32),
        compiler_params=_params(("parallel",)),
        name="merge",
    )(x, norm, w_gate, b_gate, pool, wpp, attn, wap, wout)


def _prep_w_kr(w_in_t):
    kr = w_in_t[:, N_LATENT:N_LATENT + ROPE]
    return jnp.concatenate([kr, kr], axis=1).astype(BF16)


def _prep_w_uq(w_uq):
    l, r, _ = w_uq.shape
    w = w_uq.reshape(l, r, N_HEADS, QK_DIM)
    nope = w[..., :NOPE].reshape(l, r, N_HEADS * NOPE)
    rope = w[..., NOPE:]
    return jnp.concatenate([nope, rope.reshape(l, r, N_HEADS * ROPE)], axis=-1).astype(BF16)


def _prep_w_ukv(w_ukv):
    l, r, _ = w_ukv.shape
    w = w_ukv.reshape(l, r, N_HEADS, 2, NOPE)
    return jnp.swapaxes(w, 2, 3).reshape(l, r, 2 * N_HEADS * NOPE).astype(BF16)


def kernel(x, positions, norm_ffn1, ffn1_up, ffn1_down, norm_mix, w_in, b_gate, pool_maps, pool_scale, w_pool_proj,
           q_latent_norm, w_uq, kv_latent_norm, w_ukv, w_attn_proj, w_out, norm_ffn2, ffn2_up, ffn2_down, final_norm):
    batch, seq, d = x.shape
    depth = norm_ffn1.shape[0]
    t = batch * seq
    assert seq % 128 == 0 and d == 1024

    row3 = lambda a: a.reshape(a.shape[0], 1, a.shape[1])
    w_in_t = jnp.swapaxes(w_in, 1, 2)
    n_gate = w_in_t.shape[1] - N_LATENT - ROPE
    w_kr = _prep_w_kr(w_in_t)
    wq = _prep_w_uq(w_uq)
    wkv = _prep_w_ukv(w_ukv)
    pmaps = pool_maps.astype(BF16)
    n1, nm, n2 = row3(norm_ffn1), row3(norm_mix), row3(norm_ffn2)
    qn, kvn, ps, bg = row3(q_latent_norm), row3(kv_latent_norm), row3(pool_scale), row3(b_gate)
    gf = final_norm.reshape(1, d)

    def whole(w, layer):
        return (w, layer, 0, w.shape[1])

    cos, sin = _rope_tables(positions)
    xt = x.reshape(t, d)
    w_ffn = (ffn1_up[0].astype(BF16), ffn1_down[0].astype(BF16))
    for layer in range(depth):
        casts = [whole(ffn2_up, layer), whole(ffn2_down, layer),
                 (w_in_t, layer, 0, N_LATENT), (w_in_t, layer, N_LATENT + ROPE, n_gate),
                 whole(w_pool_proj, layer), whole(w_attn_proj, layer), whole(w_out, layer)]
        xt, (*w_ffn, w_lat, w_gate, wpp, wap, wout) = _ffn(xt, n1, layer, *w_ffn, casts=casts)
        pool, *qkv = _mix_in(xt, seq, nm, w_lat, w_kr, qn, wq, kvn, wkv, pmaps, ps, cos, sin, layer)
        attn = _attention(*qkv, batch, seq)
        xt = _merge(xt, nm, w_gate, bg, pool, wpp, attn, wap, wout, layer)
        if layer == depth - 1:
            xt, _ = _ffn(xt, n2, layer, *w_ffn, final_norm=gf)
        else:
            xt, w_ffn = _ffn(xt, n2, layer, *w_ffn, casts=[whole(ffn1_up, layer + 1), whole(ffn1_down, layer + 1)])
    return xt.reshape(batch, seq, d)
```

```python
import functools
import math

import jax
import jax.numpy as jnp
from jax import lax
from jax.experimental import pallas as pl
from jax.experimental.pallas import tpu as pltpu

F32 = jnp.float32
BF16 = jnp.bfloat16

N_HEADS = 8
NOPE = 128
ROPE = 64
HALF = ROPE // 2
V_DIM = 128
QK_DIM = NOPE + ROPE
EXP2_SCALE = QK_DIM ** -0.5 * math.log2(math.e)
ROPE_THETA = 10000.0
POOL_WINDOWS = (2, 4, 8, 16)
POOL_GROUP = 128
POOL_DIM = len(POOL_WINDOWS) * POOL_GROUP
POOL_CARRY = 32
Q_LATENT = 384
KV_LATENT = 256
N_LATENT = POOL_DIM + Q_LATENT + KV_LATENT
NORM_EPS = 1e-6
LANES = 128
BF16_SUBLANES = 16
NEG_BIG = -1e30

ROPE_TABLE_BLOCK = 4096
FFN_TOKEN_TILE = 1024
FFN_FF_TILE = 256
MIX_TOKEN_TILE = 512
MIX_SUB_TILES = 2
MERGE_TOKEN_TILE = 1024
MERGE_SUB_TILES = 2
ATTN_TILE = 512
VMEM_LIMIT = 56 * 1024 * 1024


def _rms(x, g):
    ms = jnp.mean(x * x, axis=-1, keepdims=True)
    return x * lax.rsqrt(ms + NORM_EPS) * g


def _dot(a, b):
    return jnp.dot(a, b, preferred_element_type=F32)


def _dot_t(a, b):
    return lax.dot_general(a, b, (((1,), (1,)), ((), ())), preferred_element_type=F32)


def _rotate_half(x, first_half):
    return jnp.where(first_half, -pltpu.roll(x, LANES - HALF, 1), pltpu.roll(x, HALF, 1))


def _params(sem):
    return pltpu.CompilerParams(dimension_semantics=sem, vmem_limit_bytes=VMEM_LIMIT)


def _rope_table_kernel(pos_ref, invf_ref, cos_ref, sin_ref):
    rows = pos_ref.shape[0]
    groups = LANES // HALF
    ang = pos_ref[...].astype(F32) * invf_ref[...]
    group = lax.broadcasted_iota(jnp.int32, (rows, LANES), 1) // HALF
    for table, out_ref in ((jnp.cos(ang), cos_ref), (jnp.sin(ang), sin_ref)):
        rolled = [table] + [pltpu.roll(table, HALF * k, 1) for k in range(1, groups)]
        for g in range(groups):
            spread = rolled[(-g) % groups]
            for k in range(1, groups):
                spread = jnp.where(group == k, rolled[(k - g) % groups], spread)
            out_ref[g * rows:(g + 1) * rows, :] = spread


def _rope_tables(positions):
    t = positions.size
    inv_freq = ROPE_THETA ** (-jnp.arange(0, ROPE, 2, dtype=F32) / ROPE)
    groups = LANES // HALF
    block = min(t, ROPE_TABLE_BLOCK)
    rows = block // groups
    pos = positions.reshape(t // block, groups, rows)
    pos = jnp.repeat(jnp.swapaxes(pos, 1, 2), HALF, axis=2).reshape(t // groups, LANES)
    invf = jnp.tile(inv_freq, groups).reshape(1, LANES)
    return pl.pallas_call(
        _rope_table_kernel,
        grid=(t // block,),
        in_specs=[pl.BlockSpec((rows, LANES), lambda i: (i, 0)),
                  pl.BlockSpec((1, LANES), lambda i: (0, 0))],
        out_specs=[pl.BlockSpec((block, LANES), lambda i: (i, 0))] * 2,
        out_shape=[jax.ShapeDtypeStruct((t, LANES), F32)] * 2,
        compiler_params=_params(("parallel",)),
        name="rope_tables",
    )(pos, invf)


def _ffn_kernel(*refs, d_ff, tf, final, n_casts):
    x_ref, g_ref, wup_ref, wd_ref = refs[:4]
    if final:
        gf_ref, o_ref = refs[4:]
    else:
        o_ref = refs[4 + n_casts]
        for src, dst in zip(refs[4:4 + n_casts], refs[5 + n_casts:]):
            dst[...] = src[...].astype(BF16)
    x = x_ref[...]
    h = _rms(x, g_ref[...]).astype(BF16)
    acc = None
    for lo in range(0, d_ff, tf):
        gate = _dot(h, wup_ref[:, lo:lo + tf])
        up = _dot(h, wup_ref[:, d_ff + lo:d_ff + lo + tf])
        act = (gate * jax.nn.sigmoid(gate) * up).astype(BF16)
        part = _dot(act, wd_ref[lo:lo + tf, :])
        acc = part if acc is None else acc + part
    y = x + 0.5 * acc
    if final:
        y = _rms(y, gf_ref[...])
    o_ref[...] = y


def _ffn(x, norm, layer, w_up, w_down, casts=(), final_norm=None):
    t, d = x.shape
    f = w_down.shape[0]
    tm = min(FFN_TOKEN_TILE, t)
    steps = t // tm
    final = final_norm is not None
    resident = pl.Buffered(1)
    in_specs = [
        pl.BlockSpec((tm, d), lambda i: (i, 0)),
        pl.BlockSpec((None, 1, d), lambda i: (layer, 0, 0)),
        pl.BlockSpec((d, 2 * f), lambda i: (0, 0), pipeline_mode=resident),
        pl.BlockSpec((f, d), lambda i: (0, 0), pipeline_mode=resident),
    ]
    args = [x, norm, w_up, w_down]
    out_specs = [pl.BlockSpec((tm, d), lambda i: (i, 0))]
    out_shape = [jax.ShapeDtypeStruct((t, d), F32)]
    if final:
        assert not casts
        in_specs.append(pl.BlockSpec((1, d), lambda i: (0, 0)))
        args.append(final_norm)
    in_kernel = []
    for job, (src, src_layer, first_row, n_rows) in enumerate(casts):
        slabs = math.gcd(steps, n_rows // BF16_SUBLANES)
        slab_rows = n_rows // slabs
        if n_rows % (slabs * BF16_SUBLANES) or first_row % slab_rows:
            continue
        in_kernel.append(job)
        width = src.shape[2]
        in_specs.append(pl.BlockSpec(
            (None, slab_rows, width),
            lambda i, sl=src_layer, fs=first_row // slab_rows, rep=steps // slabs: (sl, fs + i // rep, 0)))
        args.append(src)
        out_specs.append(pl.BlockSpec((slab_rows, width), lambda i, rep=steps // slabs: (i // rep, 0)))
        out_shape.append(jax.ShapeDtypeStruct((n_rows, width), BF16))
    out = pl.pallas_call(
        functools.partial(_ffn_kernel, d_ff=f, tf=FFN_FF_TILE, final=final, n_casts=len(in_kernel)),
        grid=(steps,),
        in_specs=in_specs,
        out_specs=out_specs,
        out_shape=out_shape,
        compiler_params=_params(("arbitrary",)),
        name="ffn_final" if final else "ffn",
    )(*args)
    done = dict(zip(in_kernel, out[1:]))
    cast = [done[job] if job in done else src[sl, first:first + n].astype(BF16)
            for job, (src, sl, first, n) in enumerate(casts)]
    return out[0], cast


def _mix_in_kernel(x_ref, g_ref, wl_ref, wkr_ref, qn_ref, wq_ref, kvn_ref, wkv_ref, pm_ref, ps_ref, cos_ref, sin_ref,
                   pool_ref, qnope_ref, qrope_ref, knope_ref, krope_ref, v_ref, e_ref, t1_ref, t2_ref, t3_ref,
                   *, tm, tiles_per_seq):
    c = POOL_CARRY
    g = POOL_GROUP
    seq_tile = lax.rem(pl.program_id(0), tiles_per_seq)

    @pl.when(seq_tile == 0)
    def _():
        e_ref[0:c, :] = jnp.zeros((c, POOL_DIM), F32)

    ts = tm // MIX_SUB_TILES
    lane = lax.broadcasted_iota(jnp.int32, (ts, LANES), 1)
    first_half = (lane & (ROPE - 1)) < HALF
    n_all = N_HEADS * NOPE
    for r0 in range(0, tm, ts):
        rows = slice(r0, r0 + ts)
        h = _rms(x_ref[rows, :], g_ref[...]).astype(BF16)
        p = _dot_t(h, wl_ref[...])
        kr_main = _dot_t(h, wkr_ref[...])
        e_ref[c + r0:c + r0 + ts, :] = p[:, :POOL_DIM]
        q_lat = p[:, POOL_DIM:POOL_DIM + Q_LATENT]
        kv_lat = p[:, POOL_DIM + Q_LATENT:N_LATENT]
        cos = cos_ref[rows, :]
        sin = sin_ref[rows, :]

        qa = _dot(_rms(q_lat, qn_ref[...]).astype(BF16), wq_ref[...]) * EXP2_SCALE
        qnope_ref[rows, :] = qa[:, :n_all].astype(BF16)
        for pair in range(N_HEADS // 2):
            main = qa[:, n_all + pair * LANES:n_all + (pair + 1) * LANES]
            roped = main * cos + _rotate_half(main, first_half) * sin
            qrope_ref[rows, pair * LANES:(pair + 1) * LANES] = roped.astype(BF16)

        krope_ref[rows, :] = (kr_main * cos + _rotate_half(kr_main, first_half) * sin).astype(BF16)
        kv = _dot(_rms(kv_lat, kvn_ref[...]).astype(BF16), wkv_ref[...])
        knope_ref[rows, :] = kv[:, :n_all].astype(BF16)
        v_ref[rows, :] = kv[:, n_all:].astype(BF16)

    xp = e_ref[c:c + tm, :]
    t1_ref[8:c + tm, :] = e_ref[8:c + tm, :] + e_ref[7:c + tm - 1, :]
    t2_ref[16:c + tm, :] = t1_ref[16:c + tm, g:4 * g] + t1_ref[14:c + tm - 2, g:4 * g]
    t3_ref[24:c + tm, :] = t2_ref[24:c + tm, g:3 * g] + t2_ref[20:c + tm - 4, g:3 * g]
    w16 = t3_ref[c:c + tm, g:2 * g] + t3_ref[c - 8:c + tm - 8, g:2 * g]
    sums = (t1_ref[c:c + tm, 0:g], t2_ref[c:c + tm, 0:g], t3_ref[c:c + tm, 0:g], w16)
    e_ref[0:c, :] = e_ref[tm:tm + c, :]

    pos1 = (seq_tile * tm + 1 + lax.broadcasted_iota(jnp.int32, (tm, 1), 0)).astype(F32)
    for gi, w in enumerate(POOL_WINDOWS):
        count = jnp.minimum(pos1, float(w))
        pooled = sums[gi] / count - xp[:, gi * g:(gi + 1) * g]
        mixed = _dot(pooled.astype(BF16), pm_ref[gi]) * ps_ref[:, gi * g:(gi + 1) * g]
        pool_ref[:, gi * g:(gi + 1) * g] = mixed.astype(BF16)


def _mix_in(x, seq, norm, w_lat, w_kr, qn, wq, kvn, wkv, pmaps, pscale, cos, sin, layer):
    t, d = x.shape
    tm = min(MIX_TOKEN_TILE, seq)
    tiles_per_seq = seq // tm
    c = POOL_CARRY
    lay3 = lambda i: (layer, 0, 0)
    row = lambda i: (i, 0)
    in_specs = [
        pl.BlockSpec((tm, d), row),
        pl.BlockSpec((None, 1, d), lay3),
        pl.BlockSpec(w_lat.shape, lambda i: (0, 0)),
        pl.BlockSpec((None,) + w_kr.shape[1:], lay3),
        pl.BlockSpec((None, 1, qn.shape[2]), lay3),
        pl.BlockSpec((None,) + wq.shape[1:], lay3),
        pl.BlockSpec((None, 1, kvn.shape[2]), lay3),
        pl.BlockSpec((None,) + wkv.shape[1:], lay3),
        pl.BlockSpec((None,) + pmaps.shape[1:], lambda i: (layer, 0, 0, 0)),
        pl.BlockSpec((None, 1, POOL_DIM), lay3),
        pl.BlockSpec((tm, LANES), row),
        pl.BlockSpec((tm, LANES), row),
    ]
    widths = (POOL_DIM, N_HEADS * NOPE, N_HEADS * ROPE, N_HEADS * NOPE, LANES, N_HEADS * V_DIM)
    out_shape = [jax.ShapeDtypeStruct((t, w), BF16) for w in widths]
    out_specs = [pl.BlockSpec((tm, w), row) for w in widths]
    g = POOL_GROUP
    return pl.pallas_call(
        functools.partial(_mix_in_kernel, tm=tm, tiles_per_seq=tiles_per_seq),
        grid=(t // tm,),
        in_specs=in_specs,
        out_specs=out_specs,
        out_shape=out_shape,
        scratch_shapes=[pltpu.VMEM((c + tm, 4 * g), F32), pltpu.VMEM((c + tm, 4 * g), F32),
                        pltpu.VMEM((c + tm, 3 * g), F32), pltpu.VMEM((c + tm, 2 * g), F32)],
        compiler_params=_params(("arbitrary",)),
        name="mix_in",
    )(x, norm, w_lat, w_kr, qn, wq, kvn, wkv, pmaps, pscale, cos, sin)


def _attn_kernel(qnope_ref, qrope_ref, knope_ref, krope_ref, v_ref, o_ref, *, seq, tile):
    tq = tk = tile
    causal = lax.broadcasted_iota(jnp.int32, (tq, tk), 0) >= lax.broadcasted_iota(jnp.int32, (tq, tk), 1)
    lane = lax.broadcasted_iota(jnp.int32, (tq, LANES), 1)
    for qi in range(seq // tq):
        q0 = qi * tq
        q_pair = qrope_ref[q0:q0 + tq, :]
        for hh in range(2):
            cols = slice(hh * NOPE, (hh + 1) * NOPE)
            own = (lane < ROPE) if hh == 0 else (lane >= ROPE)
            q = jnp.concatenate([qnope_ref[q0:q0 + tq, cols], jnp.where(own, q_pair, jnp.zeros_like(q_pair))], axis=1)
            scores = []
            for k0 in range(0, q0 + tq, tk):
                k = jnp.concatenate([knope_ref[k0:k0 + tk, cols], krope_ref[k0:k0 + tk, :]], axis=1)
                s = _dot_t(q, k)
                scores.append(jnp.where(causal, s, NEG_BIG) if k0 == q0 else s)
            m = functools.reduce(jnp.maximum, [jnp.max(s, axis=-1, keepdims=True) for s in scores])
            l = acc = None
            for kj, s in enumerate(scores):
                p = jnp.exp2(s - m)
                p_sum = jnp.sum(p, axis=-1, keepdims=True)
                pv = _dot(p.astype(BF16), v_ref[kj * tk:(kj + 1) * tk, cols])
                l = p_sum if l is None else l + p_sum
                acc = pv if acc is None else acc + pv
            o_ref[q0:q0 + tq, cols] = (acc / l).astype(BF16)


def _attention(q_nope, q_rope, k_nope, k_rope, v, batch, seq):
    t = v.shape[0]
    tile = min(ATTN_TILE, seq)
    pair = lambda b, h: (b, h)
    return pl.pallas_call(
        functools.partial(_attn_kernel, seq=seq, tile=tile),
        grid=(batch, N_HEADS // 2),
        in_specs=[pl.BlockSpec((seq, 2 * NOPE), pair),
                  pl.BlockSpec((seq, LANES), pair),
                  pl.BlockSpec((seq, 2 * NOPE), pair),
                  pl.BlockSpec((seq, LANES), lambda b, h: (b, 0)),
                  pl.BlockSpec((seq, 2 * V_DIM), pair)],
        out_specs=pl.BlockSpec((seq, 2 * V_DIM), pair),
        out_shape=jax.ShapeDtypeStruct((t, N_HEADS * V_DIM), BF16),
        compiler_params=_params(("parallel", "parallel")),
        name="attn",
    )(q_nope, q_rope, k_nope, k_rope, v)


def _merge_kernel(x_ref, g_ref, wg_ref, bg_ref, pool_ref, wpp_ref, attn_ref, wap_ref, wout_ref, o_ref):
    tm, d = x_ref.shape
    ts = tm // MERGE_SUB_TILES
    for r0 in range(0, tm, ts):
        rows = slice(r0, r0 + ts)
        x = x_ref[rows, :]
        h = _rms(x, g_ref[...]).astype(BF16)
        gates = jax.nn.sigmoid(_dot_t(h, wg_ref[...]) + bg_ref[...])
        branch_a = _dot(pool_ref[rows, :], wpp_ref[...])
        branch_b = _dot(attn_ref[rows, :], wap_ref[...])
        merged = gates[:, :d] * branch_a + gates[:, d:] * branch_b
        o_ref[rows, :] = x + _dot(merged.astype(BF16), wout_ref[...])


def _merge(x, norm, w_gate, b_gate, pool, wpp, attn, wap, wout, layer):
    t, d = x.shape
    tm = min(MERGE_TOKEN_TILE, t)
    lay3 = lambda i: (layer, 0, 0)
    row = lambda i: (i, 0)
    whole = lambda i: (0, 0)
    resident = pl.Buffered(1)
    return pl.pallas_call(
        _merge_kernel,
        grid=(t // tm,),
        in_specs=[
            pl.BlockSpec((tm, d), row),
            pl.BlockSpec((None, 1, d), lay3),
            pl.BlockSpec(w_gate.shape, whole, pipeline_mode=resident),
            pl.BlockSpec((None, 1, 2 * d), lay3),
            pl.BlockSpec((tm, pool.shape[1]), row),
            pl.BlockSpec(wpp.shape, whole, pipeline_mode=resident),
            pl.BlockSpec((tm, attn.shape[1]), row),
            pl.BlockSpec(wap.shape, whole, pipeline_mode=resident),
            pl.BlockSpec(wout.shape, whole, pipeline_mode=resident),
        ],
        out_specs=pl.BlockSpec((tm, d), row),
        out_shape=jax.ShapeDtypeStruct((t, d), F32),
        compiler_params=_params(("parallel",)),
        name="merge",
    )(x, norm, w_gate, b_gate, pool, wpp, attn, wap, wout)


def _prep_w_kr(w_in_t):
    kr = w_in_t[:, N_LATENT:N_LATENT + ROPE]
    return jnp.concatenate([kr, kr], axis=1).astype(BF16)


def _prep_w_uq(w_uq):
    l, r, _ = w_uq.shape
    w = w_uq.reshape(l, r, N_HEADS, QK_DIM)
    nope = w[..., :NOPE].reshape(l, r, N_HEADS * NOPE)
    rope = w[..., NOPE:]
    return jnp.concatenate([nope, rope.reshape(l, r, N_HEADS * ROPE)], axis=-1).astype(BF16)


def _prep_w_ukv(w_ukv):
    l, r, _ = w_ukv.shape
    w = w_ukv.reshape(l, r, N_HEADS, 2, NOPE)
    return jnp.swapaxes(w, 2, 3).reshape(l, r, 2 * N_HEADS * NOPE).astype(BF16)


def kernel(x, positions, norm_ffn1, ffn1_up, ffn1_down, norm_mix, w_in, b_gate, pool_maps, pool_scale, w_pool_proj,
           q_latent_norm, w_uq, kv_latent_norm, w_ukv, w_attn_proj, w_out, norm_ffn2, ffn2_up, ffn2_down, final_norm):
    batch, seq, d = x.shape
    depth = norm_ffn1.shape[0]
    t = batch * seq
    assert seq % 128 == 0 and d == 1024

    row3 = lambda a: a.reshape(a.shape[0], 1, a.shape[1])
    w_in_t = jnp.swapaxes(w_in, 1, 2)
    n_gate = w_in_t.shape[1] - N_LATENT - ROPE
    w_kr = _prep_w_kr(w_in_t)
    wq = _prep_w_uq(w_uq)
    wkv = _prep_w_ukv(w_ukv)
    pmaps = pool_maps.astype(BF16)
    n1, nm, n2 = row3(norm_ffn1), row3(norm_mix), row3(norm_ffn2)
    qn, kvn, ps, bg = row3(q_latent_norm), row3(kv_latent_norm), row3(pool_scale), row3(b_gate)
    gf = final_norm.reshape(1, d)

    def whole(w, layer):
        return (w, layer, 0, w.shape[1])

    cos, sin = _rope_tables(positions)
    xt = x.reshape(t, d)
    w_ffn = (ffn1_up[0].astype(BF16), ffn1_down[0].astype(BF16))
    for layer in range(depth):
        casts = [whole(ffn2_up, layer), whole(ffn2_down, layer),
                 (w_in_t, layer, 0, N_LATENT), (w_in_t, layer, N_LATENT + ROPE, n_gate),
                 whole(w_pool_proj, layer), whole(w_attn_proj, layer), whole(w_out, layer)]
        xt, (*w_ffn, w_lat, w_gate, wpp, wap, wout) = _ffn(xt, n1, layer, *w_ffn, casts=casts)
        pool, *qkv = _mix_in(xt, seq, nm, w_lat, w_kr, qn, wq, kvn, wkv, pmaps, ps, cos, sin, layer)
        attn = _attention(*qkv, batch, seq)
        xt = _merge(xt, nm, w_gate, bg, pool, wpp, attn, wap, wout, layer)
        if layer == depth - 1:
            xt, _ = _ffn(xt, n2, layer, *w_ffn, final_norm=gf)
        else:
            xt, w_ffn = _ffn(xt, n2, layer, *w_ffn, casts=[whole(ffn1_up, layer + 1), whole(ffn1_down, layer + 1)])
    return xt.reshape(batch, seq, d)
```

```python
import functools
import math

import jax
import jax.numpy as jnp
from jax import lax
from jax.experimental import pallas as pl
from jax.experimental.pallas import tpu as pltpu

F32 = jnp.float32
BF16 = jnp.bfloat16

N_HEADS = 8
NOPE = 128
ROPE = 64
HALF = ROPE // 2
V_DIM = 128
QK_DIM = NOPE + ROPE
EXP2_SCALE = QK_DIM ** -0.5 * math.log2(math.e)
ROPE_THETA = 10000.0
POOL_WINDOWS = (2, 4, 8, 16)
POOL_GROUP = 128
POOL_DIM = len(POOL_WINDOWS) * POOL_GROUP
POOL_CARRY = 32
Q_LATENT = 384
KV_LATENT = 256
N_LATENT = POOL_DIM + Q_LATENT + KV_LATENT
NORM_EPS = 1e-6
LANES = 128
BF16_SUBLANES = 16
NEG_BIG = -1e30

ROPE_TABLE_BLOCK = 4096
FFN_TOKEN_TILE = 1024
FFN_FF_TILE = 256
MIX_TOKEN_TILE = 512
MIX_SUB_TILES = 2
MERGE_TOKEN_TILE = 1024
MERGE_SUB_TILES = 2
ATTN_TILE = 512
VMEM_LIMIT = 56 * 1024 * 1024


def _rms(x, g):
    ms = jnp.mean(x * x, axis=-1, keepdims=True)
    return x * lax.rsqrt(ms + NORM_EPS) * g


def _dot(a, b):
    return jnp.dot(a, b, preferred_element_type=F32)


def _dot_t(a, b):
    return lax.dot_general(a, b, (((1,), (1,)), ((), ())), preferred_element_type=F32)


def _rotate_half(x, first_half):
    return jnp.where(first_half, -pltpu.roll(x, LANES - HALF, 1), pltpu.roll(x, HALF, 1))


def _params(sem):
    return pltpu.CompilerParams(dimension_semantics=sem, vmem_limit_bytes=VMEM_LIMIT)


def _rope_table_kernel(pos_ref, invf_ref, cos_ref, sin_ref):
    rows = pos_ref.shape[0]
    groups = LANES // HALF
    ang = pos_ref[...].astype(F32) * invf_ref[...]
    group = lax.broadcasted_iota(jnp.int32, (rows, LANES), 1) // HALF
    for table, out_ref in ((jnp.cos(ang), cos_ref), (jnp.sin(ang), sin_ref)):
        rolled = [table] + [pltpu.roll(table, HALF * k, 1) for k in range(1, groups)]
        for g in range(groups):
            spread = rolled[(-g) % groups]
            for k in range(1, groups):
                spread = jnp.where(group == k, rolled[(k - g) % groups], spread)
            out_ref[g * rows:(g + 1) * rows, :] = spread


def _rope_tables(positions):
    t = positions.size
    inv_freq = ROPE_THETA ** (-jnp.arange(0, ROPE, 2, dtype=F32) / ROPE)
    groups = LANES // HALF
    block = min(t, ROPE_TABLE_BLOCK)
    rows = block // groups
    pos = positions.reshape(t // block, groups, rows)
    pos = jnp.repeat(jnp.swapaxes(pos, 1, 2), HALF, axis=2).reshape(t // groups, LANES)
    invf = jnp.tile(inv_freq, groups).reshape(1, LANES)
    return pl.pallas_call(
        _rope_table_kernel,
        grid=(t // block,),
        in_specs=[pl.BlockSpec((rows, LANES), lambda i: (i, 0)),
                  pl.BlockSpec((1, LANES), lambda i: (0, 0))],
        out_specs=[pl.BlockSpec((block, LANES), lambda i: (i, 0))] * 2,
        out_shape=[jax.ShapeDtypeStruct((t, LANES), F32)] * 2,
        compiler_params=_params(("parallel",)),
        name="rope_tables",
    )(pos, invf)


def _ffn_kernel(*refs, d_ff, tf, final, n_casts):
    x_ref, g_ref, wup_ref, wd_ref = refs[:4]
    if final:
        gf_ref, o_ref = refs[4:]
    else:
        o_ref = refs[4 + n_casts]
        for src, dst in zip(refs[4:4 + n_casts], refs[5 + n_casts:]):
            dst[...] = src[...].astype(BF16)
    x = x_ref[...]
    h = _rms(x, g_ref[...]).astype(BF16)
    acc = None
    for lo in range(0, d_ff, tf):
        gate = _dot(h, wup_ref[:, lo:lo + tf])
        up = _dot(h, wup_ref[:, d_ff + lo:d_ff + lo + tf])
        act = (gate * jax.nn.sigmoid(gate) * up).astype(BF16)
        part = _dot(act, wd_ref[lo:lo + tf, :])
        acc = part if acc is None else acc + part
    y = x + 0.5 * acc
    if final:
        y = _rms(y, gf_ref[...])
    o_ref[...] = y


def _ffn(x, norm, layer, w_up, w_down, casts=(), final_norm=None):
    t, d = x.shape
    f = w_down.shape[0]
    tm = min(FFN_TOKEN_TILE, t)
    steps = t // tm
    final = final_norm is not None
    resident = pl.Buffered(1)
    in_specs = [
        pl.BlockSpec((tm, d), lambda i: (i, 0)),
        pl.BlockSpec((None, 1, d), lambda i: (layer, 0, 0)),
        pl.BlockSpec((d, 2 * f), lambda i: (0, 0), pipeline_mode=resident),
        pl.BlockSpec((f, d), lambda i: (0, 0), pipeline_mode=resident),
    ]
    args = [x, norm, w_up, w_down]
    out_specs = [pl.BlockSpec((tm, d), lambda i: (i, 0))]
    out_shape = [jax.ShapeDtypeStruct((t, d), F32)]
    if final:
        assert not casts
        in_specs.append(pl.BlockSpec((1, d), lambda i: (0, 0)))
        args.append(final_norm)
    in_kernel = []
    for job, (src, src_layer, first_row, n_rows) in enumerate(casts):
        slabs = math.gcd(steps, n_rows // BF16_SUBLANES)
        slab_rows = n_rows // slabs
        if n_rows % (slabs * BF16_SUBLANES) or first_row % slab_rows:
            continue
        in_kernel.append(job)
        width = src.shape[2]
        in_specs.append(pl.BlockSpec(
            (None, slab_rows, width),
            lambda i, sl=src_layer, fs=first_row // slab_rows, rep=steps // slabs: (sl, fs + i // rep, 0)))
        args.append(src)
        out_specs.append(pl.BlockSpec((slab_rows, width), lambda i, rep=steps // slabs: (i // rep, 0)))
        out_shape.append(jax.ShapeDtypeStruct((n_rows, width), BF16))
    out = pl.pallas_call(
        functools.partial(_ffn_kernel, d_ff=f, tf=FFN_FF_TILE, final=final, n_casts=len(in_kernel)),
        grid=(steps,),
        in_specs=in_specs,
        out_specs=out_specs,
        out_shape=out_shape,
        compiler_params=_params(("arbitrary",)),
        name="ffn_final" if final else "ffn",
    )(*args)
    done = dict(zip(in_kernel, out[1:]))
    cast = [done[job] if job in done else src[sl, first:first + n].astype(BF16)
            for job, (src, sl, first, n) in enumerate(casts)]
    return out[0], cast


def _mix_in_kernel(x_ref, g_ref, wl_ref, wkr_ref, qn_ref, wq_ref, kvn_ref, wkv_ref, pm_ref, ps_ref, cos_ref, sin_ref,
                   pool_ref, qnope_ref, qrope_ref, knope_ref, krope_ref, v_ref, e_ref, t1_ref, t2_ref, t3_ref,
                   *, tm, tiles_per_seq):
    c = POOL_CARRY
    g = POOL_GROUP
    seq_tile = lax.rem(pl.program_id(0), tiles_per_seq)

    @pl.when(seq_tile == 0)
    def _():
        e_ref[0:c, :] = jnp.zeros((c, POOL_DIM), F32)

    ts = tm // MIX_SUB_TILES
    lane = lax.broadcasted_iota(jnp.int32, (ts, LANES), 1)
    first_half = (lane & (ROPE - 1)) < HALF
    n_all = N_HEADS * NOPE
    for r0 in range(0, tm, ts):
        rows = slice(r0, r0 + ts)
        h = _rms(x_ref[rows, :], g_ref[...]).astype(BF16)
        p = _dot_t(h, wl_ref[...])
        kr_main = _dot_t(h, wkr_ref[...])
        e_ref[c + r0:c + r0 + ts, :] = p[:, :POOL_DIM]
        q_lat = p[:, POOL_DIM:POOL_DIM + Q_LATENT]
        kv_lat = p[:, POOL_DIM + Q_LATENT:N_LATENT]
        cos = cos_ref[rows, :]
        sin = sin_ref[rows, :]

        qa = _dot(_rms(q_lat, qn_ref[...]).astype(BF16), wq_ref[...]) * EXP2_SCALE
        qnope_ref[rows, :] = qa[:, :n_all].astype(BF16)
        for pair in range(N_HEADS // 2):
            main = qa[:, n_all + pair * LANES:n_all + (pair + 1) * LANES]
            roped = main * cos + _rotate_half(main, first_half) * sin
            qrope_ref[rows, pair * LANES:(pair + 1) * LANES] = roped.astype(BF16)

        krope_ref[rows, :] = (kr_main * cos + _rotate_half(kr_main, first_half) * sin).astype(BF16)
        kv = _dot(_rms(kv_lat, kvn_ref[...]).astype(BF16), wkv_ref[...])
        knope_ref[rows, :] = kv[:, :n_all].astype(BF16)
        v_ref[rows, :] = kv[:, n_all:].astype(BF16)

    xp = e_ref[c:c + tm, :]
    t1_ref[8:c + tm, :] = e_ref[8:c + tm, :] + e_ref[7:c + tm - 1, :]
    t2_ref[16:c + tm, :] = t1_ref[16:c + tm, g:4 * g] + t1_ref[14:c + tm - 2, g:4 * g]
    t3_ref[24:c + tm, :] = t2_ref[24:c + tm, g:3 * g] + t2_ref[20:c + tm - 4, g:3 * g]
    w16 = t3_ref[c:c + tm, g:2 * g] + t3_ref[c - 8:c + tm - 8, g:2 * g]
    sums = (t1_ref[c:c + tm, 0:g], t2_ref[c:c + tm, 0:g], t3_ref[c:c + tm, 0:g], w16)
    e_ref[0:c, :] = e_ref[tm:tm + c, :]

    pos1 = (seq_tile * tm + 1 + lax.broadcasted_iota(jnp.int32, (tm, 1), 0)).astype(F32)
    for gi, w in enumerate(POOL_WINDOWS):
        count = jnp.minimum(pos1, float(w))
        pooled = sums[gi] / count - xp[:, gi * g:(gi + 1) * g]
        mixed = _dot(pooled.astype(BF16), pm_ref[gi]) * ps_ref[:, gi * g:(gi + 1) * g]
        pool_ref[:, gi * g:(gi + 1) * g] = mixed.astype(BF16)


def _mix_in(x, seq, norm, w_lat, w_kr, qn, wq, kvn, wkv, pmaps, pscale, cos, sin, layer):
    t, d = x.shape
    tm = min(MIX_TOKEN_TILE, seq)
    tiles_per_seq = seq // tm
    c = POOL_CARRY
    lay3 = lambda i: (layer, 0, 0)
    row = lambda i: (i, 0)
    in_specs = [
        pl.BlockSpec((tm, d), row),
        pl.BlockSpec((None, 1, d), lay3),
        pl.BlockSpec(w_lat.shape, lambda i: (0, 0)),
        pl.BlockSpec((None,) + w_kr.shape[1:], lay3),
        pl.BlockSpec((None, 1, qn.shape[2]), lay3),
        pl.BlockSpec((None,) + wq.shape[1:], lay3),
        pl.BlockSpec((None, 1, kvn.shape[2]), lay3),
        pl.BlockSpec((None,) + wkv.shape[1:], lay3),
        pl.BlockSpec((None,) + pmaps.shape[1:], lambda i: (layer, 0, 0, 0)),
        pl.BlockSpec((None, 1, POOL_DIM), lay3),
        pl.BlockSpec((tm, LANES), row),
        pl.BlockSpec((tm, LANES), row),
    ]
    widths = (POOL_DIM, N_HEADS * NOPE, N_HEADS * ROPE, N_HEADS * NOPE, LANES, N_HEADS * V_DIM)
    out_shape = [jax.ShapeDtypeStruct((t, w), BF16) for w in widths]
    out_specs = [pl.BlockSpec((tm, w), row) for w in widths]
    g = POOL_GROUP
    return pl.pallas_call(
        functools.partial(_mix_in_kernel, tm=tm, tiles_per_seq=tiles_per_seq),
        grid=(t // tm,),
        in_specs=in_specs,
        out_specs=out_specs,
        out_shape=out_shape,
        scratch_shapes=[pltpu.VMEM((c + tm, 4 * g), F32), pltpu.VMEM((c + tm, 4 * g), F32),
                        pltpu.VMEM((c + tm, 3 * g), F32), pltpu.VMEM((c + tm, 2 * g), F32)],
        compiler_params=_params(("arbitrary",)),
        name="mix_in",
    )(x, norm, w_lat, w_kr, qn, wq, kvn, wkv, pmaps, pscale, cos, sin)


def _attn_kernel(qnope_ref, qrope_ref, knope_ref, krope_ref, v_ref, o_ref, *, seq, tile):
    tq = tk = tile
    causal = lax.broadcasted_iota(jnp.int32, (tq, tk), 0) >= lax.broadcasted_iota(jnp.int32, (tq, tk), 1)
    lane = lax.broadcasted_iota(jnp.int32, (tq, LANES), 1)
    ones = jnp.ones((tk, V_DIM), BF16)
    for qi in range(seq // tq):
        q0 = qi * tq
        q_pair = qrope_ref[q0:q0 + tq, :]
        for hh in range(2):
            cols = slice(hh * NOPE, (hh + 1) * NOPE)
            own = (lane < ROPE) if hh == 0 else (lane >= ROPE)
            q = jnp.concatenate([qnope_ref[q0:q0 + tq, cols], jnp.where(own, q_pair, jnp.zeros_like(q_pair))], axis=1)
            scores = []
            for k0 in range(0, q0 + tq, tk):
                k = jnp.concatenate([knope_ref[k0:k0 + tk, cols], krope_ref[k0:k0 + tk, :]], axis=1)
                s = _dot_t(q, k)
                scores.append(jnp.where(causal, s, NEG_BIG) if k0 == q0 else s)
            m = functools.reduce(jnp.maximum, [jnp.max(s, axis=-1, keepdims=True) for s in scores])
            acc = None
            for kj, s in enumerate(scores):
                p = jnp.exp2(s - m).astype(BF16)
                v_ones = jnp.concatenate([v_ref[kj * tk:(kj + 1) * tk, cols], ones], axis=1)
                pv = _dot(p, v_ones)
                acc = pv if acc is None else acc + pv
            o_ref[q0:q0 + tq, cols] = (acc[:, :V_DIM] / acc[:, V_DIM:]).astype(BF16)


def _attention(q_nope, q_rope, k_nope, k_rope, v, batch, seq):
    t = v.shape[0]
    tile = min(ATTN_TILE, seq)
    pair = lambda b, h: (b, h)
    return pl.pallas_call(
        functools.partial(_attn_kernel, seq=seq, tile=tile),
        grid=(batch, N_HEADS // 2),
        in_specs=[pl.BlockSpec((seq, 2 * NOPE), pair),
                  pl.BlockSpec((seq, LANES), pair),
                  pl.BlockSpec((seq, 2 * NOPE), pair),
                  pl.BlockSpec((seq, LANES), lambda b, h: (b, 0)),
                  pl.BlockSpec((seq, 2 * V_DIM), pair)],
        out_specs=pl.BlockSpec((seq, 2 * V_DIM), pair),
        out_shape=jax.ShapeDtypeStruct((t, N_HEADS * V_DIM), BF16),
        compiler_params=_params(("parallel", "parallel")),
        name="attn",
    )(q_nope, q_rope, k_nope, k_rope, v)


def _merge_kernel(x_ref, g_ref, wg_ref, bg_ref, pool_ref, wpp_ref, attn_ref, wap_ref, wout_ref, o_ref):
    tm, d = x_ref.shape
    ts = tm // MERGE_SUB_TILES
    for r0 in range(0, tm, ts):
        rows = slice(r0, r0 + ts)
        x = x_ref[rows, :]
        h = _rms(x, g_ref[...]).astype(BF16)
        gates = jax.nn.sigmoid(_dot_t(h, wg_ref[...]) + bg_ref[...])
        branch_a = _dot(pool_ref[rows, :], wpp_ref[...])
        branch_b = _dot(attn_ref[rows, :], wap_ref[...])
        merged = gates[:, :d] * branch_a + gates[:, d:] * branch_b
        o_ref[rows, :] = x + _dot(merged.astype(BF16), wout_ref[...])


def _merge(x, norm, w_gate, b_gate, pool, wpp, attn, wap, wout, layer):
    t, d = x.shape
    tm = min(MERGE_TOKEN_TILE, t)
    lay3 = lambda i: (layer, 0, 0)
    row = lambda i: (i, 0)
    whole = lambda i: (0, 0)
    resident = pl.Buffered(1)
    return pl.pallas_call(
        _merge_kernel,
        grid=(t // tm,),
        in_specs=[
            pl.BlockSpec((tm, d), row),
            pl.BlockSpec((None, 1, d), lay3),
            pl.BlockSpec(w_gate.shape, whole, pipeline_mode=resident),
            pl.BlockSpec((None, 1, 2 * d), lay3),
            pl.BlockSpec((tm, pool.shape[1]), row),
            pl.BlockSpec(wpp.shape, whole, pipeline_mode=resident),
            pl.BlockSpec((tm, attn.shape[1]), row),
            pl.BlockSpec(wap.shape, whole, pipeline_mode=resident),
            pl.BlockSpec(wout.shape, whole, pipeline_mode=resident),
        ],
        out_specs=pl.BlockSpec((tm, d), row),
        out_shape=jax.ShapeDtypeStruct((t, d), F32),
        compiler_params=_params(("parallel",)),
        name="merge",
    )(x, norm, w_gate, b_gate, pool, wpp, attn, wap, wout)


def _prep_w_kr(w_in_t):
    kr = w_in_t[:, N_LATENT:N_LATENT + ROPE]
    return jnp.concatenate([kr, kr], axis=1).astype(BF16)


def _prep_w_uq(w_uq):
    l, r, _ = w_uq.shape
    w = w_uq.reshape(l, r, N_HEADS, QK_DIM)
    nope = w[..., :NOPE].reshape(l, r, N_HEADS * NOPE)
    rope = w[..., NOPE:]
    return jnp.concatenate([nope, rope.reshape(l, r, N_HEADS * ROPE)], axis=-1).astype(BF16)


def _prep_w_ukv(w_ukv):
    l, r, _ = w_ukv.shape
    w = w_ukv.reshape(l, r, N_HEADS, 2, NOPE)
    return jnp.swapaxes(w, 2, 3).reshape(l, r, 2 * N_HEADS * NOPE).astype(BF16)


def kernel(x, positions, norm_ffn1, ffn1_up, ffn1_down, norm_mix, w_in, b_gate, pool_maps, pool_scale, w_pool_proj,
           q_latent_norm, w_uq, kv_latent_norm, w_ukv, w_attn_proj, w_out, norm_ffn2, ffn2_up, ffn2_down, final_norm):
    batch, seq, d = x.shape
    depth = norm_ffn1.shape[0]
    t = batch * seq
    assert seq % 128 == 0 and d == 1024

    row3 = lambda a: a.reshape(a.shape[0], 1, a.shape[1])
    w_in_t = jnp.swapaxes(w_in, 1, 2)
    n_gate = w_in_t.shape[1] - N_LATENT - ROPE
    w_kr = _prep_w_kr(w_in_t)
    wq = _prep_w_uq(w_uq)
    wkv = _prep_w_ukv(w_ukv)
    pmaps = pool_maps.astype(BF16)
    n1, nm, n2 = row3(norm_ffn1), row3(norm_mix), row3(norm_ffn2)
    qn, kvn, ps, bg = row3(q_latent_norm), row3(kv_latent_norm), row3(pool_scale), row3(b_gate)
    gf = final_norm.reshape(1, d)

    def whole(w, layer):
        return (w, layer, 0, w.shape[1])

    cos, sin = _rope_tables(positions)
    xt = x.reshape(t, d)
    w_ffn = (ffn1_up[0].astype(BF16), ffn1_down[0].astype(BF16))
    for layer in range(depth):
        casts = [whole(ffn2_up, layer), whole(ffn2_down, layer),
                 (w_in_t, layer, 0, N_LATENT), (w_in_t, layer, N_LATENT + ROPE, n_gate),
                 whole(w_pool_proj, layer), whole(w_attn_proj, layer), whole(w_out, layer)]
        xt, (*w_ffn, w_lat, w_gate, wpp, wap, wout) = _ffn(xt, n1, layer, *w_ffn, casts=casts)
        pool, *qkv = _mix_in(xt, seq, nm, w_lat, w_kr, qn, wq, kvn, wkv, pmaps, ps, cos, sin, layer)
        attn = _attention(*qkv, batch, seq)
        xt = _merge(xt, nm, w_gate, bg, pool, wpp, attn, wap, wout, layer)
        if layer == depth - 1:
            xt, _ = _ffn(xt, n2, layer, *w_ffn, final_norm=gf)
        else:
            xt, w_ffn = _ffn(xt, n2, layer, *w_ffn, casts=[whole(ffn1_up, layer + 1), whole(ffn1_down, layer + 1)])
    return xt.reshape(batch, seq, d)
```

```python
import functools
import math

import jax
import jax.numpy as jnp
from jax import lax
from jax.experimental import pallas as pl
from jax.experimental.pallas import tpu as pltpu

F32 = jnp.float32
BF16 = jnp.bfloat16

N_HEADS = 8
NOPE = 128
ROPE = 64
HALF = ROPE // 2
V_DIM = 128
QK_DIM = NOPE + ROPE
EXP2_SCALE = QK_DIM ** -0.5 * math.log2(math.e)
ROPE_THETA = 10000.0
POOL_WINDOWS = (2, 4, 8, 16)
POOL_GROUP = 128
POOL_DIM = len(POOL_WINDOWS) * POOL_GROUP
POOL_CARRY = 32
Q_LATENT = 384
KV_LATENT = 256
N_LATENT = POOL_DIM + Q_LATENT + KV_LATENT
NORM_EPS = 1e-6
LANES = 128
BF16_SUBLANES = 16
NEG_BIG = -1e30

ROPE_TABLE_BLOCK = 4096
FFN_TOKEN_TILE = 1024
FFN_FF_TILE = 256
MIX_TOKEN_TILE = 512
MIX_SUB_TILES = 2
MERGE_TOKEN_TILE = 1024
MERGE_SUB_TILES = 2
ATTN_TILE = 512
VMEM_LIMIT = 56 * 1024 * 1024


def _rms(x, g):
    ms = jnp.mean(x * x, axis=-1, keepdims=True)
    return x * lax.rsqrt(ms + NORM_EPS) * g


def _dot(a, b):
    return jnp.dot(a, b, preferred_element_type=F32)


def _dot_t(a, b):
    return lax.dot_general(a, b, (((1,), (1,)), ((), ())), preferred_element_type=F32)


def _rotate_half(x, first_half):
    return jnp.where(first_half, -pltpu.roll(x, LANES - HALF, 1), pltpu.roll(x, HALF, 1))


def _params(sem):
    return pltpu.CompilerParams(dimension_semantics=sem, vmem_limit_bytes=VMEM_LIMIT)


def _rope_table_kernel(pos_ref, invf_ref, cos_ref, sin_ref):
    rows = pos_ref.shape[0]
    groups = LANES // HALF
    ang = pos_ref[...].astype(F32) * invf_ref[...]
    group = lax.broadcasted_iota(jnp.int32, (rows, LANES), 1) // HALF
    for table, out_ref in ((jnp.cos(ang), cos_ref), (jnp.sin(ang), sin_ref)):
        rolled = [table] + [pltpu.roll(table, HALF * k, 1) for k in range(1, groups)]
        for g in range(groups):
            spread = rolled[(-g) % groups]
            for k in range(1, groups):
                spread = jnp.where(group == k, rolled[(k - g) % groups], spread)
            out_ref[g * rows:(g + 1) * rows, :] = spread


def _rope_tables(positions):
    t = positions.size
    inv_freq = ROPE_THETA ** (-jnp.arange(0, ROPE, 2, dtype=F32) / ROPE)
    groups = LANES // HALF
    block = min(t, ROPE_TABLE_BLOCK)
    rows = block // groups
    pos = positions.reshape(t // block, groups, rows)
    pos = jnp.repeat(jnp.swapaxes(pos, 1, 2), HALF, axis=2).reshape(t // groups, LANES)
    invf = jnp.tile(inv_freq, groups).reshape(1, LANES)
    return pl.pallas_call(
        _rope_table_kernel,
        grid=(t // block,),
        in_specs=[pl.BlockSpec((rows, LANES), lambda i: (i, 0)),
                  pl.BlockSpec((1, LANES), lambda i: (0, 0))],
        out_specs=[pl.BlockSpec((block, LANES), lambda i: (i, 0))] * 2,
        out_shape=[jax.ShapeDtypeStruct((t, LANES), F32)] * 2,
        compiler_params=_params(("parallel",)),
        name="rope_tables",
    )(pos, invf)


def _ffn_kernel(*refs, d_ff, tf, final, n_casts):
    x_ref, g_ref, wup_ref, wd_ref = refs[:4]
    if final:
        gf_ref, o_ref = refs[4:]
    else:
        o_ref = refs[4 + n_casts]
        for src, dst in zip(refs[4:4 + n_casts], refs[5 + n_casts:]):
            dst[...] = src[...].astype(BF16)
    x = x_ref[...]
    h = _rms(x, g_ref[...]).astype(BF16)
    acc = None
    for lo in range(0, d_ff, tf):
        gate = _dot(h, wup_ref[:, lo:lo + tf])
        up = _dot(h, wup_ref[:, d_ff + lo:d_ff + lo + tf])
        act = (gate * jax.nn.sigmoid(gate) * up).astype(BF16)
        part = _dot(act, wd_ref[lo:lo + tf, :])
        acc = part if acc is None else acc + part
    y = x + 0.5 * acc
    if final:
        y = _rms(y, gf_ref[...])
    o_ref[...] = y


def _ffn(x, norm, layer, w_up, w_down, casts=(), final_norm=None):
    t, d = x.shape
    f = w_down.shape[0]
    tm = min(FFN_TOKEN_TILE, t)
    steps = t // tm
    final = final_norm is not None
    resident = pl.Buffered(1)
    in_specs = [
        pl.BlockSpec((tm, d), lambda i: (i, 0)),
        pl.BlockSpec((None, 1, d), lambda i: (layer, 0, 0)),
        pl.BlockSpec((d, 2 * f), lambda i: (0, 0), pipeline_mode=resident),
        pl.BlockSpec((f, d), lambda i: (0, 0), pipeline_mode=resident),
    ]
    args = [x, norm, w_up, w_down]
    out_specs = [pl.BlockSpec((tm, d), lambda i: (i, 0))]
    out_shape = [jax.ShapeDtypeStruct((t, d), F32)]
    if final:
        assert not casts
        in_specs.append(pl.BlockSpec((1, d), lambda i: (0, 0)))
        args.append(final_norm)
    in_kernel = []
    for job, (src, src_layer, first_row, n_rows) in enumerate(casts):
        slabs = math.gcd(steps, n_rows // BF16_SUBLANES)
        slab_rows = n_rows // slabs
        if n_rows % (slabs * BF16_SUBLANES) or first_row % slab_rows:
            continue
        in_kernel.append(job)
        width = src.shape[2]
        in_specs.append(pl.BlockSpec(
            (None, slab_rows, width),
            lambda i, sl=src_layer, fs=first_row // slab_rows, rep=steps // slabs: (sl, fs + i // rep, 0)))
        args.append(src)
        out_specs.append(pl.BlockSpec((slab_rows, width), lambda i, rep=steps // slabs: (i // rep, 0)))
        out_shape.append(jax.ShapeDtypeStruct((n_rows, width), BF16))
    out = pl.pallas_call(
        functools.partial(_ffn_kernel, d_ff=f, tf=FFN_FF_TILE, final=final, n_casts=len(in_kernel)),
        grid=(steps,),
        in_specs=in_specs,
        out_specs=out_specs,
        out_shape=out_shape,
        compiler_params=_params(("arbitrary",)),
        name="ffn_final" if final else "ffn",
    )(*args)
    done = dict(zip(in_kernel, out[1:]))
    cast = [done[job] if job in done else src[sl, first:first + n].astype(BF16)
            for job, (src, sl, first, n) in enumerate(casts)]
    return out[0], cast


def _mix_in_kernel(x_ref, g_ref, wl_ref, wkr_ref, qn_ref, wq_ref, kvn_ref, wkv_ref, pm_ref, ps_ref, cos_ref, sin_ref,
                   pool_ref, qnope_ref, qrope_ref, knope_ref, krope_ref, v_ref, e_ref, t1_ref, t2_ref, t3_ref,
                   *, tm, tiles_per_seq):
    c = POOL_CARRY
    g = POOL_GROUP
    seq_tile = lax.rem(pl.program_id(0), tiles_per_seq)

    @pl.when(seq_tile == 0)
    def _():
        e_ref[0:c, :] = jnp.zeros((c, POOL_DIM), F32)

    ts = tm // MIX_SUB_TILES
    lane = lax.broadcasted_iota(jnp.int32, (ts, LANES), 1)
    first_half = (lane & (ROPE - 1)) < HALF
    n_all = N_HEADS * NOPE
    for r0 in range(0, tm, ts):
        rows = slice(r0, r0 + ts)
        h = _rms(x_ref[rows, :], g_ref[...]).astype(BF16)
        p = _dot_t(h, wl_ref[...])
        kr_main = _dot_t(h, wkr_ref[...])
        e_ref[c + r0:c + r0 + ts, :] = p[:, :POOL_DIM]
        q_lat = p[:, POOL_DIM:POOL_DIM + Q_LATENT]
        kv_lat = p[:, POOL_DIM + Q_LATENT:N_LATENT]
        cos = cos_ref[rows, :]
        sin = sin_ref[rows, :]

        qa = _dot(_rms(q_lat, qn_ref[...]).astype(BF16), wq_ref[...]) * EXP2_SCALE
        qnope_ref[rows, :] = qa[:, :n_all].astype(BF16)
        for pair in range(N_HEADS // 2):
            main = qa[:, n_all + pair * LANES:n_all + (pair + 1) * LANES]
            roped = main * cos + _rotate_half(main, first_half) * sin
            qrope_ref[rows, pair * LANES:(pair + 1) * LANES] = roped.astype(BF16)

        krope_ref[rows, :] = (kr_main * cos + _rotate_half(kr_main, first_half) * sin).astype(BF16)
        kv = _dot(_rms(kv_lat, kvn_ref[...]).astype(BF16), wkv_ref[...])
        knope_ref[rows, :] = kv[:, :n_all].astype(BF16)
        v_ref[rows, :] = kv[:, n_all:].astype(BF16)

    xp = e_ref[c:c + tm, :]
    t1_ref[8:c + tm, :] = e_ref[8:c + tm, :] + e_ref[7:c + tm - 1, :]
    t2_ref[16:c + tm, :] = t1_ref[16:c + tm, g:4 * g] + t1_ref[14:c + tm - 2, g:4 * g]
    t3_ref[24:c + tm, :] = t2_ref[24:c + tm, g:3 * g] + t2_ref[20:c + tm - 4, g:3 * g]
    w16 = t3_ref[c:c + tm, g:2 * g] + t3_ref[c - 8:c + tm - 8, g:2 * g]
    sums = (t1_ref[c:c + tm, 0:g], t2_ref[c:c + tm, 0:g], t3_ref[c:c + tm, 0:g], w16)
    e_ref[0:c, :] = e_ref[tm:tm + c, :]

    pos1 = (seq_tile * tm + 1 + lax.broadcasted_iota(jnp.int32, (tm, 1), 0)).astype(F32)
    for gi, w in enumerate(POOL_WINDOWS):
        count = jnp.minimum(pos1, float(w))
        pooled = sums[gi] / count - xp[:, gi * g:(gi + 1) * g]
        mixed = _dot(pooled.astype(BF16), pm_ref[gi]) * ps_ref[:, gi * g:(gi + 1) * g]
        pool_ref[:, gi * g:(gi + 1) * g] = mixed.astype(BF16)


def _mix_in(x, seq, norm, w_lat, w_kr, qn, wq, kvn, wkv, pmaps, pscale, cos, sin, layer):
    t, d = x.shape
    tm = min(MIX_TOKEN_TILE, seq)
    tiles_per_seq = seq // tm
    c = POOL_CARRY
    lay3 = lambda i: (layer, 0, 0)
    row = lambda i: (i, 0)
    in_specs = [
        pl.BlockSpec((tm, d), row),
        pl.BlockSpec((None, 1, d), lay3),
        pl.BlockSpec(w_lat.shape, lambda i: (0, 0)),
        pl.BlockSpec((None,) + w_kr.shape[1:], lay3),
        pl.BlockSpec((None, 1, qn.shape[2]), lay3),
        pl.BlockSpec((None,) + wq.shape[1:], lay3),
        pl.BlockSpec((None, 1, kvn.shape[2]), lay3),
        pl.BlockSpec((None,) + wkv.shape[1:], lay3),
        pl.BlockSpec((None,) + pmaps.shape[1:], lambda i: (layer, 0, 0, 0)),
        pl.BlockSpec((None, 1, POOL_DIM), lay3),
        pl.BlockSpec((tm, LANES), row),
        pl.BlockSpec((tm, LANES), row),
    ]
    widths = (POOL_DIM, N_HEADS * NOPE, N_HEADS * ROPE, N_HEADS * NOPE, LANES, N_HEADS * V_DIM)
    out_shape = [jax.ShapeDtypeStruct((t, w), BF16) for w in widths]
    out_specs = [pl.BlockSpec((tm, w), row) for w in widths]
    g = POOL_GROUP
    return pl.pallas_call(
        functools.partial(_mix_in_kernel, tm=tm, tiles_per_seq=tiles_per_seq),
        grid=(t // tm,),
        in_specs=in_specs,
        out_specs=out_specs,
        out_shape=out_shape,
        scratch_shapes=[pltpu.VMEM((c + tm, 4 * g), F32), pltpu.VMEM((c + tm, 4 * g), F32),
                        pltpu.VMEM((c + tm, 3 * g), F32), pltpu.VMEM((c + tm, 2 * g), F32)],
        compiler_params=_params(("arbitrary",)),
        name="mix_in",
    )(x, norm, w_lat, w_kr, qn, wq, kvn, wkv, pmaps, pscale, cos, sin)


def _attn_kernel(qnope_ref, qrope_ref, knope_ref, krope_ref, v_ref, o_ref, *, seq, tile):
    tq = tk = tile
    causal = lax.broadcasted_iota(jnp.int32, (tq, tk), 0) >= lax.broadcasted_iota(jnp.int32, (tq, tk), 1)
    lane = lax.broadcasted_iota(jnp.int32, (tq, LANES), 1)
    for qi in range(seq // tq):
        q0 = qi * tq
        q_pair = qrope_ref[q0:q0 + tq, :]
        for hh in range(2):
            cols = slice(hh * NOPE, (hh + 1) * NOPE)
            own = (lane < ROPE) if hh == 0 else (lane >= ROPE)
            q = jnp.concatenate([qnope_ref[q0:q0 + tq, cols], jnp.where(own, q_pair, jnp.zeros_like(q_pair))], axis=1)
            scores = []
            for k0 in range(0, q0 + tq, tk):
                k = jnp.concatenate([knope_ref[k0:k0 + tk, cols], krope_ref[k0:k0 + tk, :]], axis=1)
                s = _dot_t(q, k)
                scores.append(jnp.where(causal, s, NEG_BIG) if k0 == q0 else s)
            m = functools.reduce(jnp.maximum, [jnp.max(s, axis=-1, keepdims=True) for s in scores])
            n_keys = q0 + tq
            p = jnp.concatenate([jnp.exp2(s - m).astype(BF16) for s in scores], axis=1)
            v_ones = jnp.concatenate([v_ref[0:n_keys, cols], jnp.ones((n_keys, V_DIM), BF16)], axis=1)
            acc = _dot(p, v_ones)
            o_ref[q0:q0 + tq, cols] = (acc[:, :V_DIM] / acc[:, V_DIM:]).astype(BF16)


def _attention(q_nope, q_rope, k_nope, k_rope, v, batch, seq):
    t = v.shape[0]
    tile = min(ATTN_TILE, seq)
    pair = lambda b, h: (b, h)
    return pl.pallas_call(
        functools.partial(_attn_kernel, seq=seq, tile=tile),
        grid=(batch, N_HEADS // 2),
        in_specs=[pl.BlockSpec((seq, 2 * NOPE), pair),
                  pl.BlockSpec((seq, LANES), pair),
                  pl.BlockSpec((seq, 2 * NOPE), pair),
                  pl.BlockSpec((seq, LANES), lambda b, h: (b, 0)),
                  pl.BlockSpec((seq, 2 * V_DIM), pair)],
        out_specs=pl.BlockSpec((seq, 2 * V_DIM), pair),
        out_shape=jax.ShapeDtypeStruct((t, N_HEADS * V_DIM), BF16),
        compiler_params=_params(("parallel", "parallel")),
        name="attn",
    )(q_nope, q_rope, k_nope, k_rope, v)


def _merge_kernel(x_ref, g_ref, wg_ref, bg_ref, pool_ref, wpp_ref, attn_ref, wap_ref, wout_ref, o_ref):
    tm, d = x_ref.shape
    ts = tm // MERGE_SUB_TILES
    for r0 in range(0, tm, ts):
        rows = slice(r0, r0 + ts)
        x = x_ref[rows, :]
        h = _rms(x, g_ref[...]).astype(BF16)
        gates = jax.nn.sigmoid(_dot_t(h, wg_ref[...]) + bg_ref[...])
        branch_a = _dot(pool_ref[rows, :], wpp_ref[...])
        branch_b = _dot(attn_ref[rows, :], wap_ref[...])
        merged = gates[:, :d] * branch_a + gates[:, d:] * branch_b
        o_ref[rows, :] = x + _dot(merged.astype(BF16), wout_ref[...])


def _merge(x, norm, w_gate, b_gate, pool, wpp, attn, wap, wout, layer):
    t, d = x.shape
    tm = min(MERGE_TOKEN_TILE, t)
    lay3 = lambda i: (layer, 0, 0)
    row = lambda i: (i, 0)
    whole = lambda i: (0, 0)
    resident = pl.Buffered(1)
    return pl.pallas_call(
        _merge_kernel,
        grid=(t // tm,),
        in_specs=[
            pl.BlockSpec((tm, d), row),
            pl.BlockSpec((None, 1, d), lay3),
            pl.BlockSpec(w_gate.shape, whole, pipeline_mode=resident),
            pl.BlockSpec((None, 1, 2 * d), lay3),
            pl.BlockSpec((tm, pool.shape[1]), row),
            pl.BlockSpec(wpp.shape, whole, pipeline_mode=resident),
            pl.BlockSpec((tm, attn.shape[1]), row),
            pl.BlockSpec(wap.shape, whole, pipeline_mode=resident),
            pl.BlockSpec(wout.shape, whole, pipeline_mode=resident),
        ],
        out_specs=pl.BlockSpec((tm, d), row),
        out_shape=jax.ShapeDtypeStruct((t, d), F32),
        compiler_params=_params(("parallel",)),
        name="merge",
    )(x, norm, w_gate, b_gate, pool, wpp, attn, wap, wout)


def _prep_w_kr(w_in_t):
    kr = w_in_t[:, N_LATENT:N_LATENT + ROPE]
    return jnp.concatenate([kr, kr], axis=1).astype(BF16)


def _prep_w_uq(w_uq):
    l, r, _ = w_uq.shape
    w = w_uq.reshape(l, r, N_HEADS, QK_DIM)
    nope = w[..., :NOPE].reshape(l, r, N_HEADS * NOPE)
    rope = w[..., NOPE:]
    return jnp.concatenate([nope, rope.reshape(l, r, N_HEADS * ROPE)], axis=-1).astype(BF16)


def _prep_w_ukv(w_ukv):
    l, r, _ = w_ukv.shape
    w = w_ukv.reshape(l, r, N_HEADS, 2, NOPE)
    return jnp.swapaxes(w, 2, 3).reshape(l, r, 2 * N_HEADS * NOPE).astype(BF16)


def kernel(x, positions, norm_ffn1, ffn1_up, ffn1_down, norm_mix, w_in, b_gate, pool_maps, pool_scale, w_pool_proj,
           q_latent_norm, w_uq, kv_latent_norm, w_ukv, w_attn_proj, w_out, norm_ffn2, ffn2_up, ffn2_down, final_norm):
    batch, seq, d = x.shape
    depth = norm_ffn1.shape[0]
    t = batch * seq
    assert seq % 128 == 0 and d == 1024

    row3 = lambda a: a.reshape(a.shape[0], 1, a.shape[1])
    w_in_t = jnp.swapaxes(w_in, 1, 2)
    n_gate = w_in_t.shape[1] - N_LATENT - ROPE
    w_kr = _prep_w_kr(w_in_t)
    wq = _prep_w_uq(w_uq)
    wkv = _prep_w_ukv(w_ukv)
    pmaps = pool_maps.astype(BF16)
    n1, nm, n2 = row3(norm_ffn1), row3(norm_mix), row3(norm_ffn2)
    qn, kvn, ps, bg = row3(q_latent_norm), row3(kv_latent_norm), row3(pool_scale), row3(b_gate)
    gf = final_norm.reshape(1, d)

    def whole(w, layer):
        return (w, layer, 0, w.shape[1])

    cos, sin = _rope_tables(positions)
    xt = x.reshape(t, d)
    w_ffn = (ffn1_up[0].astype(BF16), ffn1_down[0].astype(BF16))
    for layer in range(depth):
        casts = [whole(ffn2_up, layer), whole(ffn2_down, layer),
                 (w_in_t, layer, 0, N_LATENT), (w_in_t, layer, N_LATENT + ROPE, n_gate),
                 whole(w_pool_proj, layer), whole(w_attn_proj, layer), whole(w_out, layer)]
        xt, (*w_ffn, w_lat, w_gate, wpp, wap, wout) = _ffn(xt, n1, layer, *w_ffn, casts=casts)
        pool, *qkv = _mix_in(xt, seq, nm, w_lat, w_kr, qn, wq, kvn, wkv, pmaps, ps, cos, sin, layer)
        attn = _attention(*qkv, batch, seq)
        xt = _merge(xt, nm, w_gate, bg, pool, wpp, attn, wap, wout, layer)
        if layer == depth - 1:
            xt, _ = _ffn(xt, n2, layer, *w_ffn, final_norm=gf)
        else:
            xt, w_ffn = _ffn(xt, n2, layer, *w_ffn, casts=[whole(ffn1_up, layer + 1), whole(ffn1_down, layer + 1)])
    return xt.reshape(batch, seq, d)
```
